```python
import jax, jax.numpy as jnp
from jax import lax
import numpy as np

D_MODEL = 1024
BATCH = 32
SEQ = 2048
DEPTH = 1

CHUNK = 64
EPS = 1e-6

GLA_HEADS = 4
GLA_KEY = D_MODEL // 2
GLA_VAL = D_MODEL
GLA_DK = GLA_KEY // GLA_HEADS
GLA_DV = GLA_VAL // GLA_HEADS
GLA_RANK = 16
GLA_GATE_NORM = 16.0

SSD_INNER = 2 * D_MODEL
SSD_HEADDIM = 64
SSD_HEADS = SSD_INNER // SSD_HEADDIM
SSD_STATE = 128
SSD_GROUPS = 8
SSD_HPG = SSD_HEADS // SSD_GROUPS
SSD_CONV = 4
SSD_CONV_DIM = SSD_INNER + 2 * SSD_GROUPS * SSD_STATE

D_FF = 2816

N_BRANCH = 2
IN_SPLITS = (GLA_KEY, GLA_KEY, GLA_VAL, GLA_VAL, GLA_RANK,
             SSD_INNER, SSD_CONV_DIM, SSD_HEADS, N_BRANCH * D_MODEL)
IN_DIM = sum(IN_SPLITS)

kernel_name = "chunk_causal_gla_ssd_macaron_hybrid"


def rmsnorm(x, w):
    xf = x.astype(jnp.float32)
    y = xf * lax.rsqrt(jnp.mean(xf * xf, axis=-1, keepdims=True) + EPS)
    return (y * w.astype(jnp.float32)).astype(x.dtype)


def swiglu(h, w_gate, w_up, w_down):
    return (jax.nn.silu(h @ w_gate) * (h @ w_up)) @ w_down


def to_scan(t):
    return jnp.swapaxes(t, 0, 1)


def causal_dwconv(x, w, b):
    seq = x.shape[1]
    xp = jnp.pad(x, ((0, 0), (SSD_CONV - 1, 0), (0, 0)))
    y = xp[:, 0:seq] * w[0]
    for i in range(1, SSD_CONV):
        y = y + xp[:, i:i + seq] * w[i]
    return y + b


def gla_mixer(q, k, v, g, f_low, w_f_up, b_f, norm_w, w_o):
    bsz, seq, _ = q.shape
    nc = seq // CHUNK
    log_a = jax.nn.log_sigmoid((f_low @ w_f_up + b_f).astype(jnp.float32)) / GLA_GATE_NORM
    log_a = log_a.reshape(bsz, nc, CHUNK, GLA_HEADS, GLA_DK)
    cum = jnp.cumsum(log_a, axis=2)
    end = cum[:, :, -1]
    qh = (q * GLA_DK ** -0.5).reshape(bsz, nc, CHUNK, GLA_HEADS, GLA_DK)
    k_dec = k.reshape(bsz, nc, CHUNK, GLA_HEADS, GLA_DK) * jnp.exp(end[:, :, None] - cum).astype(k.dtype)
    vh = v.reshape(bsz, nc, CHUNK, GLA_HEADS, GLA_DV)
    decay = jnp.exp(end)

    def step(state, inp):
        qc, kc, vc, dc = inp
        state = state * dc[..., None]
        inter = jnp.einsum('bihd,bhde->bihe', qc, state)
        scores = jnp.einsum('bihd,bjhd->bhij', qc, kc)
        intra = jnp.einsum('bhij,bjhe->bihe', scores, vc)
        state = (state + jnp.einsum('bjhd,bjhe->bhde', kc, vc)).astype(jnp.float32)
        return state, inter + intra

    s0 = jnp.zeros((bsz, GLA_HEADS, GLA_DK, GLA_DV), jnp.float32)
    _, o = lax.scan(step, s0, (to_scan(qh), to_scan(k_dec), to_scan(vh), to_scan(decay)))
    o = jnp.swapaxes(o, 0, 1).reshape(bsz, seq, GLA_HEADS, GLA_DV)
    o = rmsnorm(o, norm_w) * jax.nn.silu(g.reshape(bsz, seq, GLA_HEADS, GLA_DV).astype(jnp.float32))
    return o.reshape(bsz, seq, GLA_VAL).astype(q.dtype) @ w_o


def ssd_mixer(z, xbc, dt_raw, conv_w, conv_b, dt_bias, a_log, d_skip, norm_w, w_o):
    bsz, seq, _ = z.shape
    nc = seq // CHUNK
    xbc = jax.nn.silu(causal_dwconv(xbc, conv_w, conv_b))
    xs, bm, cm = jnp.split(xbc, [SSD_INNER, SSD_INNER + SSD_GROUPS * SSD_STATE], axis=-1)
    dt = jax.nn.softplus(dt_raw.astype(jnp.float32) + dt_bias.astype(jnp.float32))
    da = dt * (-jnp.exp(a_log.astype(jnp.float32)))
    cum = jnp.cumsum(da.reshape(bsz, nc, CHUNK, SSD_HEADS), axis=2)
    end = cum[:, :, -1]
    wgt = (jnp.exp(end[:, :, None] - cum) * dt.reshape(bsz, nc, CHUNK, SSD_HEADS))
    wgt = wgt.reshape(bsz, nc, CHUNK, SSD_GROUPS, SSD_HPG)
    decay = jnp.exp(end).reshape(bsz, nc, SSD_GROUPS, SSD_HPG)
    xh = xs.reshape(bsz, nc, CHUNK, SSD_GROUPS, SSD_HPG, SSD_HEADDIM)
    bh = bm.reshape(bsz, nc, CHUNK, SSD_GROUPS, SSD_STATE)
    ch = cm.reshape(bsz, nc, CHUNK, SSD_GROUPS, SSD_STATE)

    def step(h, inp):
        xc, bc, cc, wc, dc = inp
        h = h * dc[..., None, None]
        y_inter = jnp.einsum('bign,bgrpn->bigrp', cc, h)
        cb = jnp.einsum('bign,bjgn->bgij', cc, bc)
        y_intra = jnp.einsum('bgij,bjgr,bjgrp->bigrp', cb, wc, xc)
        h = (h + jnp.einsum('bjgr,bjgrp,bjgn->bgrpn', wc, xc, bc)).astype(jnp.float32)
        return h, y_inter + y_intra

    h0 = jnp.zeros((bsz, SSD_GROUPS, SSD_HPG, SSD_HEADDIM, SSD_STATE), jnp.float32)
    _, y = lax.scan(step, h0, (to_scan(xh), to_scan(bh), to_scan(ch), to_scan(wgt), to_scan(decay)))
    y = jnp.swapaxes(y, 0, 1).reshape(bsz, seq, SSD_GROUPS, SSD_HPG, SSD_HEADDIM)
    y = y + d_skip.reshape(SSD_GROUPS, SSD_HPG, 1) * xs.reshape(bsz, seq, SSD_GROUPS, SSD_HPG, SSD_HEADDIM)
    y = y.reshape(bsz, seq, SSD_INNER) * jax.nn.silu(z.astype(jnp.float32))
    y = rmsnorm(y.reshape(bsz, seq, SSD_GROUPS, SSD_INNER // SSD_GROUPS),
                norm_w.reshape(SSD_GROUPS, SSD_INNER // SSD_GROUPS))
    return y.reshape(bsz, seq, SSD_INNER).astype(z.dtype) @ w_o


def setup_inputs(seed: int = 0) -> dict:
    key = jax.random.key(seed)
    ks = jax.random.split(key, 24)
    f32 = jnp.float32

    def nrm(k, shape, scale):
        return jax.random.normal(k, shape, f32) * scale

    def gain(k, shape):
        return 1.0 + 0.02 * jax.random.normal(k, shape, f32)

    L = DEPTH
    u = jax.random.uniform(ks[13], (L, SSD_HEADS), f32)
    dt0 = jnp.exp(u * (np.log(0.1) - np.log(0.001)) + np.log(0.001)).astype(f32)
    dt_bias = dt0 + jnp.log(-jnp.expm1(-dt0))
    return {
        "x": jax.random.normal(ks[0], (BATCH, SEQ, D_MODEL), f32),
        "ffn1_norm": gain(ks[1], (L, D_MODEL)),
        "ffn1_w_gate": nrm(ks[2], (L, D_MODEL, D_FF), D_MODEL ** -0.5),
        "ffn1_w_up": nrm(ks[3], (L, D_MODEL, D_FF), D_MODEL ** -0.5),
        "ffn1_w_down": nrm(ks[4], (L, D_FF, D_MODEL), D_FF ** -0.5),
        "mix_norm": gain(ks[5], (L, D_MODEL)),
        "w_in": nrm(ks[6], (L, D_MODEL, IN_DIM), D_MODEL ** -0.5),
        "gla_w_f_up": nrm(ks[7], (L, GLA_RANK, GLA_KEY), GLA_RANK ** -0.5),
        "gla_b_f": nrm(ks[8], (L, GLA_KEY), 0.02),
        "gla_norm": gain(ks[9], (L, GLA_DV)),
        "gla_w_o": nrm(ks[10], (L, GLA_VAL, D_MODEL), GLA_VAL ** -0.5),
        "ssd_conv_w": nrm(ks[11], (L, SSD_CONV, SSD_CONV_DIM), SSD_CONV ** -0.5),
        "ssd_conv_b": nrm(ks[12], (L, SSD_CONV_DIM), 0.01),
        "ssd_dt_bias": dt_bias,
        "ssd_a_log": jnp.log(jax.random.uniform(ks[14], (L, SSD_HEADS), f32, 1.0, 16.0)),
        "ssd_d": gain(ks[15], (L, SSD_HEADS)),
        "ssd_norm": gain(ks[16], (L, SSD_INNER)),
        "ssd_w_o": nrm(ks[17], (L, SSD_INNER, D_MODEL), SSD_INNER ** -0.5),
        "w_out": nrm(ks[18], (L, D_MODEL, D_MODEL), D_MODEL ** -0.5),
        "ffn2_norm": gain(ks[19], (L, D_MODEL)),
        "ffn2_w_gate": nrm(ks[20], (L, D_MODEL, D_FF), D_MODEL ** -0.5),
        "ffn2_w_up": nrm(ks[21], (L, D_MODEL, D_FF), D_MODEL ** -0.5),
        "ffn2_w_down": nrm(ks[22], (L, D_FF, D_MODEL), D_FF ** -0.5),
        "final_norm": gain(ks[23], (D_MODEL,)),
    }


def reference(x, ffn1_norm, ffn1_w_gate, ffn1_w_up, ffn1_w_down, mix_norm, w_in,
              gla_w_f_up, gla_b_f, gla_norm, gla_w_o, ssd_conv_w, ssd_conv_b,
              ssd_dt_bias, ssd_a_log, ssd_d, ssd_norm, ssd_w_o, w_out,
              ffn2_norm, ffn2_w_gate, ffn2_w_up, ffn2_w_down, final_norm):
    bsz, seq, _ = x.shape
    split_idx = list(np.cumsum(IN_SPLITS)[:-1])
    for l in range(DEPTH):
        h = rmsnorm(x, ffn1_norm[l])
        x = x + 0.5 * swiglu(h, ffn1_w_gate[l], ffn1_w_up[l], ffn1_w_down[l])

        h = rmsnorm(x, mix_norm[l])
        proj = h @ w_in[l]
        gq, gk, gv, gg, gf, sz, sxbc, sdt, gates = jnp.split(proj, split_idx, axis=-1)
        u_a = gla_mixer(gq, gk, gv, gg, gf, gla_w_f_up[l], gla_b_f[l], gla_norm[l], gla_w_o[l])
        u_b = ssd_mixer(sz, sxbc, sdt, ssd_conv_w[l], ssd_conv_b[l], ssd_dt_bias[l],
                        ssd_a_log[l], ssd_d[l], ssd_norm[l], ssd_w_o[l])
        gts = jax.nn.sigmoid(gates.astype(jnp.float32)).reshape(bsz, seq, N_BRANCH, D_MODEL)
        merged = gts[:, :, 0] * u_a + gts[:, :, 1] * u_b
        x = x + merged.astype(x.dtype) @ w_out[l]

        h = rmsnorm(x, ffn2_norm[l])
        x = x + 0.5 * swiglu(h, ffn2_w_gate[l], ffn2_w_up[l], ffn2_w_down[l])
    return rmsnorm(x, final_norm)
```

```python
import functools

import jax
import jax.numpy as jnp
from jax import lax
from jax.experimental import pallas as pl
from jax.experimental.pallas import tpu as pltpu

F32 = jnp.float32
BF16 = jnp.bfloat16

D_MODEL = 1024
CHUNK = 64
EPS = 1e-6

GLA_HEADS = 4
GLA_KEY = 512
GLA_VAL = 1024
GLA_DK = 128
GLA_DV = 256
GLA_RANK = 16
GLA_GATE_NORM = 16.0

SSD_INNER = 2048
SSD_HEADDIM = 64
SSD_HEADS = 32
SSD_STATE = 128
SSD_GROUPS = 8
SSD_GROUP_W = SSD_INNER // SSD_GROUPS
SSD_CONV = 4
SSD_CONV_DIM = 4096
SSD_BC = SSD_GROUPS * SSD_STATE

D_FF = 2816

P_Z = 0
P_GATES = 2048
P_XBC = 4096
P_V = 8192
P_G = 9216
P_Q = 10240
P_K = 10752
P_WIDTH = 11264
SMALL_W = 128
SM_F = 0
SM_DT = GLA_RANK

VMEM_LIMIT = 56 * 1024 * 1024

SEQ_TILE = 512
NCHUNK = SEQ_TILE // CHUNK


def _resident(shape):
    nd = len(shape)
    return pl.BlockSpec(shape, lambda *_: (0,) * nd, pipeline_mode=pl.Buffered(1))


def _rmsnorm(x, w):
    return x * lax.rsqrt(jnp.mean(x * x, axis=-1, keepdims=True) + EPS) * w


def _softplus(x):
    return jnp.maximum(x, 0.0) + jnp.log1p(jnp.exp(-jnp.abs(x)))


def _silu(x):
    return x * jax.nn.sigmoid(x)


def _dot(a, b):
    return jnp.dot(a, b, preferred_element_type=F32)


def _dot_t0(a, b):
    return lax.dot_general(a, b, (((0,), (0,)), ((), ())), preferred_element_type=F32)


def _split2(x):
    hi = x.astype(BF16)
    lo = (x - hi.astype(F32)).astype(BF16)
    return hi, lo


def _split3(x):
    hi = x.astype(BF16)
    r = x - hi.astype(F32)
    mid = r.astype(BF16)
    lo = (r - mid.astype(F32)).astype(BF16)
    return hi, mid, lo


def _cumsum_chunk(tri, x):
    hi, lo = _split2(x)
    return _dot(tri, hi) + _dot(tri, lo)


def _ffn_body(x_ref, nw_ref, wgu_ref, wd_ref, *rest, final_norm):
    if final_norm:
        fw_ref, o_ref = rest
    else:
        (o_ref,) = rest
    x = x_ref[...]
    h = _rmsnorm(x, nw_ref[...]).astype(BF16)
    gu = _dot(h, wgu_ref[...])
    g = gu[:, :D_FF]
    u = gu[:, D_FF:]
    act = (_silu(g) * u).astype(BF16)
    out = x + 0.5 * _dot(act, wd_ref[...])
    if final_norm:
        out = _rmsnorm(out, fw_ref[...])
    o_ref[...] = out


def _ffn(x2d, norm_w, w_gate, w_up, w_down, final_w=None, tm=512):
    m = x2d.shape[0]
    wgu = jnp.concatenate([w_gate, w_up], axis=1).astype(BF16)
    wd = w_down.astype(BF16)
    ins = [x2d, norm_w.reshape(1, D_MODEL), wgu, wd]
    specs = [pl.BlockSpec((tm, D_MODEL), lambda i: (i, 0)),
             _resident((1, D_MODEL)), _resident((D_MODEL, 2 * D_FF)), _resident((D_FF, D_MODEL))]
    if final_w is not None:
        ins.append(final_w.reshape(1, D_MODEL))
        specs.append(_resident((1, D_MODEL)))
    return pl.pallas_call(
        functools.partial(_ffn_body, final_norm=final_w is not None),
        grid=(m // tm,),
        in_specs=specs,
        out_specs=pl.BlockSpec((tm, D_MODEL), lambda i: (i, 0)),
        out_shape=jax.ShapeDtypeStruct((m, D_MODEL), F32),
        compiler_params=pltpu.CompilerParams(dimension_semantics=("parallel",),
                                             vmem_limit_bytes=VMEM_LIMIT),
        name="ffn_final" if final_w is not None else "ffn",
    )(*ins)


_PROJ_COL_STEP = 2048


def _inproj_body(x_ref, nw_ref, w_ref, ws_ref, p_ref, s_ref):
    h = _rmsnorm(x_ref[...], nw_ref[...]).astype(BF16)
    for a in range(0, P_WIDTH, _PROJ_COL_STEP):
        b = min(a + _PROJ_COL_STEP, P_WIDTH)
        p_ref[:, a:b] = _dot(h, w_ref[:, a:b]).astype(BF16)
    s_ref[...] = _dot(h, ws_ref[...])


def _in_proj(x2d, norm_w, w_in, tm=256):
    m = x2d.shape[0]
    o = [0]
    for s in (GLA_KEY, GLA_KEY, GLA_VAL, GLA_VAL, GLA_RANK, SSD_INNER, SSD_CONV_DIM, SSD_HEADS,
              2 * D_MODEL):
        o.append(o[-1] + s)
    wq, wk, wv, wg, wf, wz, wxbc, wdt, wgates = (w_in[:, o[i]:o[i + 1]] for i in range(9))
    w_main = jnp.concatenate([wz, wgates, wxbc, wv, wg, wq, wk], axis=1).astype(BF16)
    w_small = jnp.concatenate(
        [wf, wdt, jnp.zeros((D_MODEL, SMALL_W - GLA_RANK - SSD_HEADS), F32)], axis=1).astype(BF16)
    return pl.pallas_call(
        _inproj_body,
        grid=(m // tm,),
        in_specs=[pl.BlockSpec((tm, D_MODEL), lambda i: (i, 0)),
                  _resident((1, D_MODEL)), _resident((D_MODEL, P_WIDTH)),
                  _resident((D_MODEL, SMALL_W))],
        out_specs=[pl.BlockSpec((tm, P_WIDTH), lambda i: (i, 0)),
                   pl.BlockSpec((tm, SMALL_W), lambda i: (i, 0))],
        out_shape=[jax.ShapeDtypeStruct((m, P_WIDTH), BF16),
                   jax.ShapeDtypeStruct((m, SMALL_W), F32)],
        compiler_params=pltpu.CompilerParams(dimension_semantics=("parallel",),
                                             vmem_limit_bytes=VMEM_LIMIT),
        name="in_proj",
    )(x2d, norm_w.reshape(1, D_MODEL), w_main, w_small)


def _gla_body(q_ref, k_ref, v_ref, g_ref, sm_ref, wf_ref, bf_ref, nw_ref, tri_ref,
              o_ref, s_ref, kd_ref, oacc_ref):
    @pl.when(pl.program_id(1) == 0)
    def _():
        s_ref[...] = jnp.zeros_like(s_ref)

    tri = tri_ref[...]
    pre = _dot(sm_ref[...].astype(BF16), wf_ref[...]) + bf_ref[...]
    log_a = -_softplus(-pre) * (1.0 / GLA_GATE_NORM)
    ends = []
    for c in range(NCHUNK):
        rows = slice(c * CHUNK, (c + 1) * CHUNK)
        cum = _cumsum_chunk(tri, log_a[rows])
        end = cum[CHUNK - 1:CHUNK]
        ends.append(end)
        kd_ref[rows, :] = (k_ref[rows, :].astype(F32) * jnp.exp(end - cum)).astype(BF16)
    dec_t = jnp.exp(jnp.concatenate(ends, axis=0)).T

    for c in range(NCHUNK):
        rows = slice(c * CHUNK, (c + 1) * CHUNK)
        for h in range(GLA_HEADS):
            kc = slice(h * GLA_DK, (h + 1) * GLA_DK)
            vc = slice(h * GLA_DV, (h + 1) * GLA_DV)
            upd = _dot_t0(kd_ref[rows, kc], v_ref[rows, vc])
            s_new = s_ref[h] * dec_t[kc, c:c + 1] + upd
            s_ref[h] = s_new
            oacc_ref[rows, vc] = _dot(q_ref[rows, kc], s_new.astype(BF16))

    scale = GLA_DK ** -0.5
    for h in range(GLA_HEADS):
        vc = slice(h * GLA_DV, (h + 1) * GLA_DV)
        o = _rmsnorm(oacc_ref[:, vc] * scale, nw_ref[...])
        o_ref[:, vc] = (o * _silu(g_ref[:, vc].astype(F32))).astype(BF16)


def _gla(p, small, w_f_up, b_f, norm_w, tri, bsz, seq):
    t = SEQ_TILE
    nt = seq // t
    wf = jnp.concatenate([w_f_up, jnp.zeros((SMALL_W - GLA_RANK, GLA_KEY), F32)], axis=0).astype(BF16)

    def col(width, off):
        blk = off // width
        return pl.BlockSpec((t, width), lambda b, j: (b * nt + j, blk))

    return pl.pallas_call(
        _gla_body,
        grid=(bsz, nt),
        in_specs=[col(GLA_KEY, P_Q), col(GLA_KEY, P_K), col(GLA_VAL, P_V), col(GLA_VAL, P_G),
                  pl.BlockSpec((t, SMALL_W), lambda b, j: (b * nt + j, 0)),
                  _resident((SMALL_W, GLA_KEY)), _resident((1, GLA_KEY)), _resident((1, GLA_DV)),
                  _resident((CHUNK, CHUNK))],
        out_specs=pl.BlockSpec((t, GLA_VAL), lambda b, j: (b * nt + j, 0)),
        out_shape=jax.ShapeDtypeStruct((bsz * seq, GLA_VAL), BF16),
        scratch_shapes=[pltpu.VMEM((GLA_HEADS, GLA_DK, GLA_DV), F32),
                        pltpu.VMEM((t, GLA_KEY), BF16),
                        pltpu.VMEM((t, GLA_VAL), F32)],
        compiler_params=pltpu.CompilerParams(dimension_semantics=("parallel", "arbitrary"),
                                             vmem_limit_bytes=VMEM_LIMIT),
        name="gla",
    )(p, p, p, p, small, wf, b_f.reshape(1, GLA_KEY), norm_w.reshape(1, GLA_DV), tri)


_CONV_COL_STEP = 512
_TAIL = 8


def _ssd_body(z_ref, xbc_ref, sm_ref, cw_ref, cb_ref, dtb_ref, alog_ref, e_ref, d_ref, nw_ref,
              tri_ref, o_ref, s_ref, ext_ref, xs_ref, bm_ref, cm_ref, v_ref, y_ref):
    t = SEQ_TILE

    @pl.when(pl.program_id(1) == 0)
    def _():
        s_ref[...] = jnp.zeros_like(s_ref)
        ext_ref[0:_TAIL, :] = jnp.zeros((_TAIL, SSD_CONV_DIM), F32)

    ext_ref[_TAIL:_TAIL + t, :] = xbc_ref[...].astype(F32)
    for a in range(0, SSD_CONV_DIM, _CONV_COL_STEP):
        cs = slice(a, a + _CONV_COL_STEP)
        acc = ext_ref[_TAIL - 3:_TAIL - 3 + t, cs] * cw_ref[0:1, cs]
        for i in range(1, SSD_CONV):
            acc = acc + ext_ref[_TAIL - 3 + i:_TAIL - 3 + i + t, cs] * cw_ref[i:i + 1, cs]
        act = _silu(acc + cb_ref[:, cs])
        if a < SSD_INNER:
            xs_ref[:, cs] = act
        elif a < SSD_INNER + SSD_BC:
            bm_ref[:, a - SSD_INNER:a - SSD_INNER + _CONV_COL_STEP] = act.astype(BF16)
        else:
            off = a - SSD_INNER - SSD_BC
            cm_ref[:, off:off + _CONV_COL_STEP] = act.astype(BF16)
    ext_ref[0:_TAIL, :] = ext_ref[t:t + _TAIL, :]

    lane = lax.broadcasted_iota(jnp.int32, (1, SMALL_W), 1)
    is_dt = (lane >= SM_DT) & (lane < SM_DT + SSD_HEADS)
    dt = jnp.where(is_dt, _softplus(sm_ref[...] + dtb_ref[...]), 0.0)
    a_neg = jnp.where(is_dt, -jnp.exp(alog_ref[...]), 0.0)
    da = dt * a_neg
    tri = tri_ref[...]
    e = e_ref[...]
    ends = []
    for c in range(NCHUNK):
        rows = slice(c * CHUNK, (c + 1) * CHUNK)
        cum = _cumsum_chunk(tri, da[rows])
        end = cum[CHUNK - 1:CHUNK]
        ends.append(end)
        w_hi, w_lo = _split2(jnp.exp(end - cum) * dt[rows])
        w_exp = _dot(w_hi, e) + _dot(w_lo, e)
        v_ref[rows, :] = (w_exp * xs_ref[rows, :]).astype(BF16)
    d_hi, d_mid, d_lo = _split3(jnp.exp(jnp.concatenate(ends, axis=0)))
    dec = _dot(d_hi, e) + _dot(d_mid, e) + _dot(d_lo, e)

    for c in range(NCHUNK):
        rows = slice(c * CHUNK, (c + 1) * CHUNK)
        for g in range(SSD_GROUPS):
            nc = slice(g * SSD_STATE, (g + 1) * SSD_STATE)
            vc = slice(g * SSD_GROUP_W, (g + 1) * SSD_GROUP_W)
            upd = _dot_t0(bm_ref[rows, nc], v_ref[rows, vc])
            s_new = s_ref[g] * dec[c:c + 1, vc] + upd
            s_ref[g] = s_new
            y_ref[rows, vc] = _dot(cm_ref[rows, nc], s_new.astype(BF16))

    for g in range(SSD_GROUPS):
        vc = slice(g * SSD_GROUP_W, (g + 1) * SSD_GROUP_W)
        y = y_ref[:, vc] + d_ref[:, vc] * xs_ref[:, vc]
        y = y * _silu(z_ref[:, vc].astype(F32))
        o_ref[:, vc] = _rmsnorm(y, nw_ref[:, vc]).astype(BF16)


def _ssd(p, small, conv_w, conv_b, dt_bias, a_log, d_skip, norm_w, tri, bsz, seq):
    t = SEQ_TILE
    nt = seq // t
    pad_l = jnp.zeros((SM_DT,), F32)
    pad_r = jnp.zeros((SMALL_W - SM_DT - SSD_HEADS,), F32)
    dtb = jnp.concatenate([pad_l, dt_bias, pad_r]).reshape(1, SMALL_W)
    alog = jnp.concatenate([pad_l, a_log, pad_r]).reshape(1, SMALL_W)
    head_of_col = jnp.arange(SSD_INNER, dtype=jnp.int32) // SSD_HEADDIM
    expand = (jnp.arange(SMALL_W, dtype=jnp.int32)[:, None] == head_of_col[None, :] + SM_DT).astype(BF16)
    d_cols = jnp.repeat(d_skip, SSD_HEADDIM).reshape(1, SSD_INNER)

    row = lambda b, j: (b * nt + j, 0)
    return pl.pallas_call(
        _ssd_body,
        grid=(bsz, nt),
        in_specs=[pl.BlockSpec((t, SSD_INNER), lambda b, j: (b * nt + j, P_Z // SSD_INNER)),
                  pl.BlockSpec((t, SSD_CONV_DIM), lambda b, j: (b * nt + j, P_XBC // SSD_CONV_DIM)),
                  pl.BlockSpec((t, SMALL_W), row),
                  _resident((SSD_CONV, SSD_CONV_DIM)), _resident((1, SSD_CONV_DIM)),
                  _resident((1, SMALL_W)), _resident((1, SMALL_W)),
                  _resident((SMALL_W, SSD_INNER)), _resident((1, SSD_INNER)),
                  _resident((1, SSD_INNER)), _resident((CHUNK, CHUNK))],
        out_specs=pl.BlockSpec((t, SSD_INNER), row),
        out_shape=jax.ShapeDtypeStruct((bsz * seq, SSD_INNER), BF16),
        scratch_shapes=[pltpu.VMEM((SSD_GROUPS, SSD_STATE, SSD_GROUP_W), F32),
                        pltpu.VMEM((t + _TAIL, SSD_CONV_DIM), F32),
                        pltpu.VMEM((t, SSD_INNER), F32),
                        pltpu.VMEM((t, SSD_BC), BF16),
                        pltpu.VMEM((t, SSD_BC), BF16),
                        pltpu.VMEM((t, SSD_INNER), BF16),
                        pltpu.VMEM((t, SSD_INNER), F32)],
        compiler_params=pltpu.CompilerParams(dimension_semantics=("parallel", "arbitrary"),
                                             vmem_limit_bytes=VMEM_LIMIT),
        name="ssd",
    )(p, p, small, conv_w, conv_b.reshape(1, SSD_CONV_DIM), dtb, alog, expand, d_cols,
      norm_w.reshape(1, SSD_INNER), tri)


def _merge_body(x_ref, oa_ref, yb_ref, gt_ref, wa_ref, wb_ref, wo_ref, o_ref):
    u_a = _dot(oa_ref[...], wa_ref[...])
    u_b = _dot(yb_ref[...], wb_ref[...])
    merged = (jax.nn.sigmoid(gt_ref[:, :D_MODEL].astype(F32)) * u_a
              + jax.nn.sigmoid(gt_ref[:, D_MODEL:].astype(F32)) * u_b)
    o_ref[...] = x_ref[...] + _dot(merged.astype(BF16), wo_ref[...])


def _merge(x2d, o_a, y_b, p, gla_w_o, ssd_w_o, w_out, tm=512):
    m = x2d.shape[0]
    row = lambda i: (i, 0)
    return pl.pallas_call(
        _merge_body,
        grid=(m // tm,),
        in_specs=[pl.BlockSpec((tm, D_MODEL), row),
                  pl.BlockSpec((tm, GLA_VAL), row),
                  pl.BlockSpec((tm, SSD_INNER), row),
                  pl.BlockSpec((tm, 2 * D_MODEL), lambda i: (i, P_GATES // (2 * D_MODEL))),
                  _resident((GLA_VAL, D_MODEL)), _resident((SSD_INNER, D_MODEL)),
                  _resident((D_MODEL, D_MODEL))],
        out_specs=pl.BlockSpec((tm, D_MODEL), row),
        out_shape=jax.ShapeDtypeStruct((m, D_MODEL), F32),
        compiler_params=pltpu.CompilerParams(dimension_semantics=("parallel",),
                                             vmem_limit_bytes=VMEM_LIMIT),
        name="merge",
    )(x2d, o_a, y_b, p, gla_w_o.astype(BF16), ssd_w_o.astype(BF16), w_out.astype(BF16))


def kernel(x, ffn1_norm, ffn1_w_gate, ffn1_w_up, ffn1_w_down, mix_norm, w_in, gla_w_f_up, gla_b_f, gla_norm, gla_w_o, ssd_conv_w, ssd_conv_b, ssd_dt_bias, ssd_a_log, ssd_d, ssd_norm, ssd_w_o, w_out, ffn2_norm, ffn2_w_gate, ffn2_w_up, ffn2_w_down, final_norm):
    bsz, seq, d = x.shape
    depth = ffn1_norm.shape[0]
    assert d == D_MODEL and seq % SEQ_TILE == 0
    tri = (jnp.arange(CHUNK)[:, None] >= jnp.arange(CHUNK)[None, :]).astype(BF16)
    h = x.reshape(bsz * seq, d)
    for l in range(depth):
        last = l == depth - 1
        h = _ffn(h, ffn1_norm[l], ffn1_w_gate[l], ffn1_w_up[l], ffn1_w_down[l])
        p, small = _in_proj(h, mix_norm[l], w_in[l])
        o_a = _gla(p, small, gla_w_f_up[l], gla_b_f[l], gla_norm[l], tri, bsz, seq)
        y_b = _ssd(p, small, ssd_conv_w[l], ssd_conv_b[l], ssd_dt_bias[l], ssd_a_log[l],
                   ssd_d[l], ssd_norm[l], tri, bsz, seq)
        h = _merge(h, o_a, y_b, p, gla_w_o[l], ssd_w_o[l], w_out[l])
        h = _ffn(h, ffn2_norm[l], ffn2_w_gate[l], ffn2_w_up[l], ffn2_w_down[l],
                 final_w=final_norm if last else None)
    if depth == 0:
        h = h
    return h.reshape(bsz, seq, d)
```

```python
import functools

import jax
import jax.numpy as jnp
from jax import lax
from jax.experimental import pallas as pl
from jax.experimental.pallas import tpu as pltpu

F32 = jnp.float32
BF16 = jnp.bfloat16

D_MODEL = 1024
CHUNK = 64
EPS = 1e-6

GLA_HEADS = 4
GLA_KEY = 512
GLA_VAL = 1024
GLA_DK = 128
GLA_DV = 256
GLA_RANK = 16
GLA_GATE_NORM = 16.0

SSD_INNER = 2048
SSD_HEADDIM = 64
SSD_HEADS = 32
SSD_STATE = 128
SSD_GROUPS = 8
SSD_GROUP_W = SSD_INNER // SSD_GROUPS
SSD_CONV = 4
SSD_CONV_DIM = 4096
SSD_BC = SSD_GROUPS * SSD_STATE

D_FF = 2816

P_Z = 0
P_GATES = 2048
P_XBC = 4096
P_V = 8192
P_G = 9216
P_Q = 10240
P_K = 10752
P_WIDTH = 11264
SMALL_W = 128
SM_F = 0
SM_DT = GLA_RANK

VMEM_LIMIT = 56 * 1024 * 1024

SEQ_TILE = 512
NCHUNK = SEQ_TILE // CHUNK
_LOOKAHEAD = 2


def _resident(shape):
    nd = len(shape)
    return pl.BlockSpec(shape, lambda *_: (0,) * nd, pipeline_mode=pl.Buffered(1))


def _rmsnorm(x, w):
    return x * lax.rsqrt(jnp.mean(x * x, axis=-1, keepdims=True) + EPS) * w


def _softplus(x):
    return jnp.maximum(x, 0.0) + jnp.log1p(jnp.exp(-jnp.abs(x)))


def _silu(x):
    return x * jax.nn.sigmoid(x)


def _dot(a, b):
    return jnp.dot(a, b, preferred_element_type=F32)


def _dot_t0(a, b):
    return lax.dot_general(a, b, (((0,), (0,)), ((), ())), preferred_element_type=F32)


def _split2(x):
    hi = x.astype(BF16)
    lo = (x - hi.astype(F32)).astype(BF16)
    return hi, lo


def _split3(x):
    hi = x.astype(BF16)
    r = x - hi.astype(F32)
    mid = r.astype(BF16)
    lo = (r - mid.astype(F32)).astype(BF16)
    return hi, mid, lo


def _cumsum_chunk(tri, x):
    hi, lo = _split2(x)
    return _dot(tri, hi) + _dot(tri, lo)


def _ffn_body(x_ref, nw_ref, wgu_ref, wd_ref, *rest, final_norm):
    if final_norm:
        fw_ref, o_ref = rest
    else:
        (o_ref,) = rest
    x = x_ref[...]
    h = _rmsnorm(x, nw_ref[...]).astype(BF16)
    gu = _dot(h, wgu_ref[...])
    g = gu[:, :D_FF]
    u = gu[:, D_FF:]
    act = (_silu(g) * u).astype(BF16)
    out = x + 0.5 * _dot(act, wd_ref[...])
    if final_norm:
        out = _rmsnorm(out, fw_ref[...])
    o_ref[...] = out


def _ffn(x2d, norm_w, w_gate, w_up, w_down, final_w=None, tm=512):
    m = x2d.shape[0]
    wgu = jnp.concatenate([w_gate, w_up], axis=1).astype(BF16)
    wd = w_down.astype(BF16)
    ins = [x2d, norm_w.reshape(1, D_MODEL), wgu, wd]
    specs = [pl.BlockSpec((tm, D_MODEL), lambda i: (i, 0)),
             _resident((1, D_MODEL)), _resident((D_MODEL, 2 * D_FF)), _resident((D_FF, D_MODEL))]
    if final_w is not None:
        ins.append(final_w.reshape(1, D_MODEL))
        specs.append(_resident((1, D_MODEL)))
    return pl.pallas_call(
        functools.partial(_ffn_body, final_norm=final_w is not None),
        grid=(m // tm,),
        in_specs=specs,
        out_specs=pl.BlockSpec((tm, D_MODEL), lambda i: (i, 0)),
        out_shape=jax.ShapeDtypeStruct((m, D_MODEL), F32),
        compiler_params=pltpu.CompilerParams(dimension_semantics=("parallel",),
                                             vmem_limit_bytes=VMEM_LIMIT),
        name="ffn_final" if final_w is not None else "ffn",
    )(*ins)


_PROJ_COL_STEP = 2048


def _inproj_body(x_ref, nw_ref, w_ref, ws_ref, p_ref, s_ref):
    h = _rmsnorm(x_ref[...], nw_ref[...]).astype(BF16)
    for a in range(0, P_WIDTH, _PROJ_COL_STEP):
        b = min(a + _PROJ_COL_STEP, P_WIDTH)
        p_ref[:, a:b] = _dot(h, w_ref[:, a:b]).astype(BF16)
    s_ref[...] = _dot(h, ws_ref[...])


def _in_proj(x2d, norm_w, w_in, tm=256):
    m = x2d.shape[0]
    o = [0]
    for s in (GLA_KEY, GLA_KEY, GLA_VAL, GLA_VAL, GLA_RANK, SSD_INNER, SSD_CONV_DIM, SSD_HEADS,
              2 * D_MODEL):
        o.append(o[-1] + s)
    wq, wk, wv, wg, wf, wz, wxbc, wdt, wgates = (w_in[:, o[i]:o[i + 1]] for i in range(9))
    w_main = jnp.concatenate([wz, wgates, wxbc, wv, wg, wq, wk], axis=1).astype(BF16)
    w_small = jnp.concatenate(
        [wf, wdt, jnp.zeros((D_MODEL, SMALL_W - GLA_RANK - SSD_HEADS), F32)], axis=1).astype(BF16)
    return pl.pallas_call(
        _inproj_body,
        grid=(m // tm,),
        in_specs=[pl.BlockSpec((tm, D_MODEL), lambda i: (i, 0)),
                  _resident((1, D_MODEL)), _resident((D_MODEL, P_WIDTH)),
                  _resident((D_MODEL, SMALL_W))],
        out_specs=[pl.BlockSpec((tm, P_WIDTH), lambda i: (i, 0)),
                   pl.BlockSpec((tm, SMALL_W), lambda i: (i, 0))],
        out_shape=[jax.ShapeDtypeStruct((m, P_WIDTH), BF16),
                   jax.ShapeDtypeStruct((m, SMALL_W), F32)],
        compiler_params=pltpu.CompilerParams(dimension_semantics=("parallel",),
                                             vmem_limit_bytes=VMEM_LIMIT),
        name="in_proj",
    )(x2d, norm_w.reshape(1, D_MODEL), w_main, w_small)


def _gla_body(q_ref, k_ref, v_ref, g_ref, sm_ref, wf_ref, bf_ref, nw_ref, tri_ref,
              o_ref, s_ref):
    @pl.when(pl.program_id(1) == 0)
    def _():
        s_ref[...] = jnp.zeros_like(s_ref)

    tri = tri_ref[...]
    pre = _dot(sm_ref[...].astype(BF16), wf_ref[...]) + bf_ref[...]
    log_a = -_softplus(-pre) * (1.0 / GLA_GATE_NORM)
    ends = []
    kd = []
    for c in range(NCHUNK):
        rows = slice(c * CHUNK, (c + 1) * CHUNK)
        cum = _cumsum_chunk(tri, log_a[rows])
        end = cum[CHUNK - 1:CHUNK]
        ends.append(end)
        kd.append((k_ref[rows, :].astype(F32) * jnp.exp(end - cum)).astype(BF16))
    dec_t = jnp.exp(jnp.concatenate(ends, axis=0)).T

    def key_cols(h):
        return slice(h * GLA_DK, (h + 1) * GLA_DK)

    def val_cols(h):
        return slice(h * GLA_DV, (h + 1) * GLA_DV)

    upd = {}
    dec_b = {}

    def issue_updates(c):
        rows = slice(c * CHUNK, (c + 1) * CHUNK)
        for h in range(GLA_HEADS):
            upd[c, h] = _dot_t0(kd[c][:, key_cols(h)], v_ref[rows, val_cols(h)])
            dec_b[c, h] = jnp.broadcast_to(dec_t[key_cols(h), c:c + 1], (GLA_DK, GLA_DV))

    state = [s_ref[h] for h in range(GLA_HEADS)]
    scale = GLA_DK ** -0.5
    for c in range(min(_LOOKAHEAD, NCHUNK)):
        issue_updates(c)
    for c in range(NCHUNK):
        rows = slice(c * CHUNK, (c + 1) * CHUNK)
        if c + _LOOKAHEAD < NCHUNK:
            issue_updates(c + _LOOKAHEAD)
        for h in range(GLA_HEADS):
            state[h] = state[h] * dec_b.pop((c, h)) + upd.pop((c, h))
            o = _dot(q_ref[rows, key_cols(h)], state[h].astype(BF16)) * scale
            o = _rmsnorm(o, nw_ref[...])
            o_ref[rows, val_cols(h)] = (o * _silu(g_ref[rows, val_cols(h)].astype(F32))).astype(BF16)
    for h in range(GLA_HEADS):
        s_ref[h] = state[h]


def _gla(p, small, w_f_up, b_f, norm_w, tri, bsz, seq):
    t = SEQ_TILE
    nt = seq // t
    wf = jnp.concatenate([w_f_up, jnp.zeros((SMALL_W - GLA_RANK, GLA_KEY), F32)], axis=0).astype(BF16)

    def col(width, off):
        blk = off // width
        return pl.BlockSpec((t, width), lambda b, j: (b * nt + j, blk))

    return pl.pallas_call(
        _gla_body,
        grid=(bsz, nt),
        in_specs=[col(GLA_KEY, P_Q), col(GLA_KEY, P_K), col(GLA_VAL, P_V), col(GLA_VAL, P_G),
                  pl.BlockSpec((t, SMALL_W), lambda b, j: (b * nt + j, 0)),
                  _resident((SMALL_W, GLA_KEY)), _resident((1, GLA_KEY)), _resident((1, GLA_DV)),
                  _resident((CHUNK, CHUNK))],
        out_specs=pl.BlockSpec((t, GLA_VAL), lambda b, j: (b * nt + j, 0)),
        out_shape=jax.ShapeDtypeStruct((bsz * seq, GLA_VAL), BF16),
        scratch_shapes=[pltpu.VMEM((GLA_HEADS, GLA_DK, GLA_DV), F32)],
        compiler_params=pltpu.CompilerParams(dimension_semantics=("parallel", "arbitrary"),
                                             vmem_limit_bytes=VMEM_LIMIT),
        name="gla",
    )(p, p, p, p, small, wf, b_f.reshape(1, GLA_KEY), norm_w.reshape(1, GLA_DV), tri)


_CONV_COL_STEP = 512
_TAIL = 8


def _ssd_body(z_ref, xbc_ref, sm_ref, cw_ref, cb_ref, dtb_ref, alog_ref, e_ref, d_ref, nw_ref,
              tri_ref, o_ref, s_ref, ext_ref, xs_ref, bm_ref, cm_ref, v_ref):
    t = SEQ_TILE

    @pl.when(pl.program_id(1) == 0)
    def _():
        s_ref[...] = jnp.zeros_like(s_ref)
        ext_ref[0:_TAIL, :] = jnp.zeros((_TAIL, SSD_CONV_DIM), F32)

    ext_ref[_TAIL:_TAIL + t, :] = xbc_ref[...].astype(F32)
    for a in range(0, SSD_CONV_DIM, _CONV_COL_STEP):
        cs = slice(a, a + _CONV_COL_STEP)
        acc = ext_ref[_TAIL - 3:_TAIL - 3 + t, cs] * cw_ref[0:1, cs]
        for i in range(1, SSD_CONV):
            acc = acc + ext_ref[_TAIL - 3 + i:_TAIL - 3 + i + t, cs] * cw_ref[i:i + 1, cs]
        act = _silu(acc + cb_ref[:, cs])
        if a < SSD_INNER:
            xs_ref[:, cs] = act
        elif a < SSD_INNER + SSD_BC:
            bm_ref[:, a - SSD_INNER:a - SSD_INNER + _CONV_COL_STEP] = act.astype(BF16)
        else:
            off = a - SSD_INNER - SSD_BC
            cm_ref[:, off:off + _CONV_COL_STEP] = act.astype(BF16)
    ext_ref[0:_TAIL, :] = ext_ref[t:t + _TAIL, :]

    lane = lax.broadcasted_iota(jnp.int32, (1, SMALL_W), 1)
    is_dt = (lane >= SM_DT) & (lane < SM_DT + SSD_HEADS)
    dt = jnp.where(is_dt, _softplus(sm_ref[...] + dtb_ref[...]), 0.0)
    a_neg = jnp.where(is_dt, -jnp.exp(alog_ref[...]), 0.0)
    da = dt * a_neg
    tri = tri_ref[...]
    e = e_ref[...]
    ends = []
    for c in range(NCHUNK):
        rows = slice(c * CHUNK, (c + 1) * CHUNK)
        cum = _cumsum_chunk(tri, da[rows])
        end = cum[CHUNK - 1:CHUNK]
        ends.append(end)
        w_hi, w_lo = _split2(jnp.exp(end - cum) * dt[rows])
        w_exp = _dot(w_hi, e) + _dot(w_lo, e)
        v_ref[rows, :] = (w_exp * xs_ref[rows, :]).astype(BF16)
    d_hi, d_mid, d_lo = _split3(jnp.exp(jnp.concatenate(ends, axis=0)))
    dec = _dot(d_hi, e) + _dot(d_mid, e) + _dot(d_lo, e)

    def state_cols(g):
        return slice(g * SSD_STATE, (g + 1) * SSD_STATE)

    def val_cols(g):
        return slice(g * SSD_GROUP_W, (g + 1) * SSD_GROUP_W)

    upd = {}

    def issue_updates(c):
        rows = slice(c * CHUNK, (c + 1) * CHUNK)
        for g in range(SSD_GROUPS):
            upd[c, g] = _dot_t0(bm_ref[rows, state_cols(g)], v_ref[rows, val_cols(g)])

    state = [s_ref[g] for g in range(SSD_GROUPS)]
    for c in range(min(_LOOKAHEAD, NCHUNK)):
        issue_updates(c)
    for c in range(NCHUNK):
        rows = slice(c * CHUNK, (c + 1) * CHUNK)
        if c + _LOOKAHEAD < NCHUNK:
            issue_updates(c + _LOOKAHEAD)
        for g in range(SSD_GROUPS):
            vc = val_cols(g)
            state[g] = state[g] * dec[c:c + 1, vc] + upd.pop((c, g))
            y = _dot(cm_ref[rows, state_cols(g)], state[g].astype(BF16))
            y = y + d_ref[:, vc] * xs_ref[rows, vc]
            y = y * _silu(z_ref[rows, vc].astype(F32))
            o_ref[rows, vc] = _rmsnorm(y, nw_ref[:, vc]).astype(BF16)
    for g in range(SSD_GROUPS):
        s_ref[g] = state[g]


def _ssd(p, small, conv_w, conv_b, dt_bias, a_log, d_skip, norm_w, tri, bsz, seq):
    t = SEQ_TILE
    nt = seq // t
    pad_l = jnp.zeros((SM_DT,), F32)
    pad_r = jnp.zeros((SMALL_W - SM_DT - SSD_HEADS,), F32)
    dtb = jnp.concatenate([pad_l, dt_bias, pad_r]).reshape(1, SMALL_W)
    alog = jnp.concatenate([pad_l, a_log, pad_r]).reshape(1, SMALL_W)
    head_of_col = jnp.arange(SSD_INNER, dtype=jnp.int32) // SSD_HEADDIM
    expand = (jnp.arange(SMALL_W, dtype=jnp.int32)[:, None] == head_of_col[None, :] + SM_DT).astype(BF16)
    d_cols = jnp.repeat(d_skip, SSD_HEADDIM).reshape(1, SSD_INNER)

    row = lambda b, j: (b * nt + j, 0)
    return pl.pallas_call(
        _ssd_body,
        grid=(bsz, nt),
        in_specs=[pl.BlockSpec((t, SSD_INNER), lambda b, j: (b * nt + j, P_Z // SSD_INNER)),
                  pl.BlockSpec((t, SSD_CONV_DIM), lambda b, j: (b * nt + j, P_XBC // SSD_CONV_DIM)),
                  pl.BlockSpec((t, SMALL_W), row),
                  _resident((SSD_CONV, SSD_CONV_DIM)), _resident((1, SSD_CONV_DIM)),
                  _resident((1, SMALL_W)), _resident((1, SMALL_W)),
                  _resident((SMALL_W, SSD_INNER)), _resident((1, SSD_INNER)),
                  _resident((1, SSD_INNER)), _resident((CHUNK, CHUNK))],
        out_specs=pl.BlockSpec((t, SSD_INNER), row),
        out_shape=jax.ShapeDtypeStruct((bsz * seq, SSD_INNER), BF16),
        scratch_shapes=[pltpu.VMEM((SSD_GROUPS, SSD_STATE, SSD_GROUP_W), F32),
                        pltpu.VMEM((t + _TAIL, SSD_CONV_DIM), F32),
                        pltpu.VMEM((t, SSD_INNER), F32),
                        pltpu.VMEM((t, SSD_BC), BF16),
                        pltpu.VMEM((t, SSD_BC), BF16),
                        pltpu.VMEM((t, SSD_INNER), BF16)],
        compiler_params=pltpu.CompilerParams(dimension_semantics=("parallel", "arbitrary"),
                                             vmem_limit_bytes=VMEM_LIMIT),
        name="ssd",
    )(p, p, small, conv_w, conv_b.reshape(1, SSD_CONV_DIM), dtb, alog, expand, d_cols,
      norm_w.reshape(1, SSD_INNER), tri)


def _merge_body(x_ref, oa_ref, yb_ref, gt_ref, wa_ref, wb_ref, wo_ref, o_ref):
    u_a = _dot(oa_ref[...], wa_ref[...])
    u_b = _dot(yb_ref[...], wb_ref[...])
    merged = (jax.nn.sigmoid(gt_ref[:, :D_MODEL].astype(F32)) * u_a
              + jax.nn.sigmoid(gt_ref[:, D_MODEL:].astype(F32)) * u_b)
    o_ref[...] = x_ref[...] + _dot(merged.astype(BF16), wo_ref[...])


def _merge(x2d, o_a, y_b, p, gla_w_o, ssd_w_o, w_out, tm=512):
    m = x2d.shape[0]
    row = lambda i: (i, 0)
    return pl.pallas_call(
        _merge_body,
        grid=(m // tm,),
        in_specs=[pl.BlockSpec((tm, D_MODEL), row),
                  pl.BlockSpec((tm, GLA_VAL), row),
                  pl.BlockSpec((tm, SSD_INNER), row),
                  pl.BlockSpec((tm, 2 * D_MODEL), lambda i: (i, P_GATES // (2 * D_MODEL))),
                  _resident((GLA_VAL, D_MODEL)), _resident((SSD_INNER, D_MODEL)),
                  _resident((D_MODEL, D_MODEL))],
        out_specs=pl.BlockSpec((tm, D_MODEL), row),
        out_shape=jax.ShapeDtypeStruct((m, D_MODEL), F32),
        compiler_params=pltpu.CompilerParams(dimension_semantics=("parallel",),
                                             vmem_limit_bytes=VMEM_LIMIT),
        name="merge",
    )(x2d, o_a, y_b, p, gla_w_o.astype(BF16), ssd_w_o.astype(BF16), w_out.astype(BF16))


def kernel(x, ffn1_norm, ffn1_w_gate, ffn1_w_up, ffn1_w_down, mix_norm, w_in, gla_w_f_up, gla_b_f, gla_norm, gla_w_o, ssd_conv_w, ssd_conv_b, ssd_dt_bias, ssd_a_log, ssd_d, ssd_norm, ssd_w_o, w_out, ffn2_norm, ffn2_w_gate, ffn2_w_up, ffn2_w_down, final_norm):
    bsz, seq, d = x.shape
    depth = ffn1_norm.shape[0]
    assert d == D_MODEL and seq % SEQ_TILE == 0 and depth >= 1
    tri = (jnp.arange(CHUNK)[:, None] >= jnp.arange(CHUNK)[None, :]).astype(BF16)
    h = x.reshape(bsz * seq, d)
    for l in range(depth):
        last = l == depth - 1
        h = _ffn(h, ffn1_norm[l], ffn1_w_gate[l], ffn1_w_up[l], ffn1_w_down[l])
        p, small = _in_proj(h, mix_norm[l], w_in[l])
        o_a = _gla(p, small, gla_w_f_up[l], gla_b_f[l], gla_norm[l], tri, bsz, seq)
        y_b = _ssd(p, small, ssd_conv_w[l], ssd_conv_b[l], ssd_dt_bias[l], ssd_a_log[l],
                   ssd_d[l], ssd_norm[l], tri, bsz, seq)
        h = _merge(h, o_a, y_b, p, gla_w_o[l], ssd_w_o[l], w_out[l])
        h = _ffn(h, ffn2_norm[l], ffn2_w_gate[l], ffn2_w_up[l], ffn2_w_down[l],
                 final_w=final_norm if last else None)
    return h.reshape(bsz, seq, d)
```

```python
import functools

import jax
import jax.numpy as jnp
from jax import lax
from jax.experimental import pallas as pl
from jax.experimental.pallas import tpu as pltpu

F32 = jnp.float32
BF16 = jnp.bfloat16

D_MODEL = 1024
CHUNK = 64
EPS = 1e-6

GLA_HEADS = 4
GLA_KEY = 512
GLA_VAL = 1024
GLA_DK = 128
GLA_DV = 256
GLA_RANK = 16
GLA_GATE_NORM = 16.0

SSD_INNER = 2048
SSD_HEADDIM = 64
SSD_HEADS = 32
SSD_STATE = 128
SSD_GROUPS = 8
SSD_GROUP_W = SSD_INNER // SSD_GROUPS
SSD_CONV = 4
SSD_CONV_DIM = 4096
SSD_BC = SSD_GROUPS * SSD_STATE

D_FF = 2816

P_Z = 0
P_GATES = 2048
P_V = 4096
P_G = 5120
P_Q = 6144
P_K = 6656
P_WIDTH = 7168
SMALL_W = 128
SM_F = 0
SM_DT = GLA_RANK

VMEM_LIMIT = 56 * 1024 * 1024

SEQ_TILE = 512
NCHUNK = SEQ_TILE // CHUNK
_LOOKAHEAD = 2


def _resident(shape):
    nd = len(shape)
    return pl.BlockSpec(shape, lambda *_: (0,) * nd, pipeline_mode=pl.Buffered(1))


def _rmsnorm(x, w):
    return x * lax.rsqrt(jnp.mean(x * x, axis=-1, keepdims=True) + EPS) * w


def _softplus(x):
    return jnp.maximum(x, 0.0) + jnp.log1p(jnp.exp(-jnp.abs(x)))


def _silu(x):
    return x * jax.nn.sigmoid(x)


def _dot(a, b):
    return jnp.dot(a, b, preferred_element_type=F32)


def _dot_t0(a, b):
    return lax.dot_general(a, b, (((0,), (0,)), ((), ())), preferred_element_type=F32)


def _split2(x):
    hi = x.astype(BF16)
    lo = (x - hi.astype(F32)).astype(BF16)
    return hi, lo


def _split3(x):
    hi = x.astype(BF16)
    r = x - hi.astype(F32)
    mid = r.astype(BF16)
    lo = (r - mid.astype(F32)).astype(BF16)
    return hi, mid, lo


def _cumsum_chunk(tri2, x):
    hi, lo = _split2(x)
    return _dot(tri2, jnp.concatenate([hi, lo], axis=0))


def _ffn_body(x_ref, nw_ref, wgu_ref, wd_ref, *rest, final_norm):
    if final_norm:
        fw_ref, o_ref = rest
    else:
        (o_ref,) = rest
    x = x_ref[...]
    h = _rmsnorm(x, nw_ref[...]).astype(BF16)
    gu = _dot(h, wgu_ref[...])
    g = gu[:, :D_FF]
    u = gu[:, D_FF:]
    act = (_silu(g) * u).astype(BF16)
    out = x + 0.5 * _dot(act, wd_ref[...])
    if final_norm:
        out = _rmsnorm(out, fw_ref[...])
    o_ref[...] = out


def _ffn(x2d, norm_w, w_gate, w_up, w_down, final_w=None, tm=512):
    m = x2d.shape[0]
    wgu = jnp.concatenate([w_gate, w_up], axis=1).astype(BF16)
    wd = w_down.astype(BF16)
    ins = [x2d, norm_w.reshape(1, D_MODEL), wgu, wd]
    specs = [pl.BlockSpec((tm, D_MODEL), lambda i: (i, 0)),
             _resident((1, D_MODEL)), _resident((D_MODEL, 2 * D_FF)), _resident((D_FF, D_MODEL))]
    if final_w is not None:
        ins.append(final_w.reshape(1, D_MODEL))
        specs.append(_resident((1, D_MODEL)))
    return pl.pallas_call(
        functools.partial(_ffn_body, final_norm=final_w is not None),
        grid=(m // tm,),
        in_specs=specs,
        out_specs=pl.BlockSpec((tm, D_MODEL), lambda i: (i, 0)),
        out_shape=jax.ShapeDtypeStruct((m, D_MODEL), F32),
        compiler_params=pltpu.CompilerParams(dimension_semantics=("parallel",),
                                             vmem_limit_bytes=VMEM_LIMIT),
        name="ffn_final" if final_w is not None else "ffn",
    )(*ins)


_PROJ_COL_STEP = 512
_PROJ_TILE = 256
_CONV_COL_STEP = 256
_TAIL = 8


def _inproj_body(x_ref, nw_ref, w_ref, ws_ref, wx_ref, cw_ref, cb_ref,
                 p_ref, s_ref, xs_ref, bc_ref, tail_ref, *, tiles_per_row):
    tm = _PROJ_TILE

    @pl.when(lax.rem(pl.program_id(0), tiles_per_row) == 0)
    def _():
        tail_ref[...] = jnp.zeros_like(tail_ref)

    h = _rmsnorm(x_ref[...], nw_ref[...]).astype(BF16)
    row = lax.broadcasted_iota(jnp.int32, (_TAIL, _CONV_COL_STEP), 0)

    def slab_block(a):
        p_ref[:, a:a + _PROJ_COL_STEP] = _dot(h, w_ref[:, a:a + _PROJ_COL_STEP]).astype(BF16)

    def xbc_block(a):
        cs = slice(a, a + _CONV_COL_STEP)
        r = _dot(h, wx_ref[:, cs])
        tail = tail_ref[:, cs]
        acc = None
        for i in range(SSD_CONV):
            k = SSD_CONV - 1 - i
            if k == 0:
                shifted = r
            else:
                rolled = pltpu.roll(r, k, axis=0)
                head = jnp.where(row < k, pltpu.roll(tail, k, axis=0), rolled[0:_TAIL])
                shifted = jnp.concatenate([head, rolled[_TAIL:]], axis=0)
            term = shifted * cw_ref[i:i + 1, cs]
            acc = term if acc is None else acc + term
        tail_ref[:, cs] = r[tm - _TAIL:tm]
        act = _silu(acc + cb_ref[:, cs])
        if a < SSD_INNER:
            xs_ref[:, cs] = act
        else:
            bc_ref[:, a - SSD_INNER:a - SSD_INNER + _CONV_COL_STEP] = act.astype(BF16)

    plain = [functools.partial(slab_block, a) for a in range(0, P_WIDTH, _PROJ_COL_STEP)]
    fused = [functools.partial(xbc_block, a) for a in range(0, SSD_CONV_DIM, _CONV_COL_STEP)]
    done = 0
    for i, task in enumerate(fused):
        task()
        want = ((i + 1) * len(plain)) // len(fused)
        for j in range(done, want):
            plain[j]()
        done = want
    s_ref[...] = _dot(h, ws_ref[...])


def _in_proj(x2d, norm_w, w_in, conv_w, conv_b, seq):
    m = x2d.shape[0]
    tm = _PROJ_TILE
    o = [0]
    for s in (GLA_KEY, GLA_KEY, GLA_VAL, GLA_VAL, GLA_RANK, SSD_INNER, SSD_CONV_DIM, SSD_HEADS,
              2 * D_MODEL):
        o.append(o[-1] + s)
    wq, wk, wv, wg, wf, wz, wxbc, wdt, wgates = (w_in[:, o[i]:o[i + 1]] for i in range(9))
    w_main = jnp.concatenate([wz, wgates, wv, wg, wq, wk], axis=1).astype(BF16)
    w_small = jnp.concatenate(
        [wf, wdt, jnp.zeros((D_MODEL, SMALL_W - GLA_RANK - SSD_HEADS), F32)], axis=1).astype(BF16)
    row = lambda i: (i, 0)
    return pl.pallas_call(
        functools.partial(_inproj_body, tiles_per_row=seq // tm),
        grid=(m // tm,),
        in_specs=[pl.BlockSpec((tm, D_MODEL), row),
                  _resident((1, D_MODEL)), _resident((D_MODEL, P_WIDTH)),
                  _resident((D_MODEL, SMALL_W)), _resident((D_MODEL, SSD_CONV_DIM)),
                  _resident((SSD_CONV, SSD_CONV_DIM)), _resident((1, SSD_CONV_DIM))],
        out_specs=[pl.BlockSpec((tm, P_WIDTH), row),
                   pl.BlockSpec((tm, SMALL_W), row),
                   pl.BlockSpec((tm, SSD_INNER), row),
                   pl.BlockSpec((tm, 2 * SSD_BC), row)],
        out_shape=[jax.ShapeDtypeStruct((m, P_WIDTH), BF16),
                   jax.ShapeDtypeStruct((m, SMALL_W), F32),
                   jax.ShapeDtypeStruct((m, SSD_INNER), F32),
                   jax.ShapeDtypeStruct((m, 2 * SSD_BC), BF16)],
        scratch_shapes=[pltpu.VMEM((_TAIL, SSD_CONV_DIM), F32)],
        compiler_params=pltpu.CompilerParams(dimension_semantics=("arbitrary",),
                                             vmem_limit_bytes=VMEM_LIMIT),
        name="in_proj_conv",
    )(x2d, norm_w.reshape(1, D_MODEL), w_main, w_small, wxbc.astype(BF16), conv_w,
      conv_b.reshape(1, SSD_CONV_DIM))


def _gla_body(q_ref, k_ref, v_ref, g_ref, sm_ref, wf_ref, bf_ref, nw_ref, tri_ref,
              o_ref, s_ref):
    @pl.when(pl.program_id(1) == 0)
    def _():
        s_ref[...] = jnp.zeros_like(s_ref)

    tri = tri_ref[...]
    pre = _dot(sm_ref[...].astype(BF16), wf_ref[...]) + bf_ref[...]
    log_a = -_softplus(-pre) * (1.0 / GLA_GATE_NORM)
    ends = []
    kd = []
    for c in range(NCHUNK):
        rows = slice(c * CHUNK, (c + 1) * CHUNK)
        cum = _cumsum_chunk(tri, log_a[rows])
        end = cum[CHUNK - 1:CHUNK]
        ends.append(end)
        kd.append((k_ref[rows, :].astype(F32) * jnp.exp(end - cum)).astype(BF16))
    dec_t = jnp.exp(jnp.concatenate(ends, axis=0)).T

    def key_cols(h):
        return slice(h * GLA_DK, (h + 1) * GLA_DK)

    def val_cols(h):
        return slice(h * GLA_DV, (h + 1) * GLA_DV)

    upd = {}
    dec_b = {}

    def issue_updates(c):
        rows = slice(c * CHUNK, (c + 1) * CHUNK)
        for h in range(GLA_HEADS):
            upd[c, h] = _dot_t0(kd[c][:, key_cols(h)], v_ref[rows, val_cols(h)])
            dec_b[c, h] = jnp.broadcast_to(dec_t[key_cols(h), c:c + 1], (GLA_DK, GLA_DV))

    state = [s_ref[h] for h in range(GLA_HEADS)]
    scale = GLA_DK ** -0.5
    for c in range(min(_LOOKAHEAD, NCHUNK)):
        issue_updates(c)
    for c in range(NCHUNK):
        rows = slice(c * CHUNK, (c + 1) * CHUNK)
        if c + _LOOKAHEAD < NCHUNK:
            issue_updates(c + _LOOKAHEAD)
        for h in range(GLA_HEADS):
            state[h] = state[h] * dec_b.pop((c, h)) + upd.pop((c, h))
            o = _dot(q_ref[rows, key_cols(h)], state[h].astype(BF16)) * scale
            o = _rmsnorm(o, nw_ref[...])
            o_ref[rows, val_cols(h)] = (o * _silu(g_ref[rows, val_cols(h)].astype(F32))).astype(BF16)
    for h in range(GLA_HEADS):
        s_ref[h] = state[h]


def _gla(p, small, w_f_up, b_f, norm_w, tri, bsz, seq):
    t = SEQ_TILE
    nt = seq // t
    wf = jnp.concatenate([w_f_up, jnp.zeros((SMALL_W - GLA_RANK, GLA_KEY), F32)], axis=0).astype(BF16)

    def col(width, off):
        blk = off // width
        return pl.BlockSpec((t, width), lambda b, j: (b * nt + j, blk))

    return pl.pallas_call(
        _gla_body,
        grid=(bsz, nt),
        in_specs=[col(GLA_KEY, P_Q), col(GLA_KEY, P_K), col(GLA_VAL, P_V), col(GLA_VAL, P_G),
                  pl.BlockSpec((t, SMALL_W), lambda b, j: (b * nt + j, 0)),
                  _resident((SMALL_W, GLA_KEY)), _resident((1, GLA_KEY)), _resident((1, GLA_DV)),
                  _resident((CHUNK, 2 * CHUNK))],
        out_specs=pl.BlockSpec((t, GLA_VAL), lambda b, j: (b * nt + j, 0)),
        out_shape=jax.ShapeDtypeStruct((bsz * seq, GLA_VAL), BF16),
        scratch_shapes=[pltpu.VMEM((GLA_HEADS, GLA_DK, GLA_DV), F32)],
        compiler_params=pltpu.CompilerParams(dimension_semantics=("parallel", "arbitrary"),
                                             vmem_limit_bytes=VMEM_LIMIT),
        name="gla",
    )(p, p, p, p, small, wf, b_f.reshape(1, GLA_KEY), norm_w.reshape(1, GLA_DV), tri)


def _ssd_body(z_ref, xs_ref, bc_ref, sm_ref, dtb_ref, alog_ref, e_ref, d_ref, nw_ref,
              tri_ref, o_ref, s_ref, v_ref):
    @pl.when(pl.program_id(1) == 0)
    def _():
        s_ref[...] = jnp.zeros_like(s_ref)

    lane = lax.broadcasted_iota(jnp.int32, (1, SMALL_W), 1)
    is_dt = (lane >= SM_DT) & (lane < SM_DT + SSD_HEADS)
    dt = jnp.where(is_dt, _softplus(sm_ref[...] + dtb_ref[...]), 0.0)
    a_neg = jnp.where(is_dt, -jnp.exp(alog_ref[...]), 0.0)
    da = dt * a_neg
    tri = tri_ref[...]
    e2 = e_ref[...]
    ends = []
    wgt = []
    for c in range(NCHUNK):
        rows = slice(c * CHUNK, (c + 1) * CHUNK)
        cum = _cumsum_chunk(tri, da[rows])
        end = cum[CHUNK - 1:CHUNK]
        ends.append(end)
        wgt.append(jnp.exp(end - cum) * dt[rows])
    w_hi, w_lo = _split2(jnp.concatenate(wgt, axis=0))
    w_exp = _dot(jnp.concatenate([w_hi, w_lo], axis=1), e2)
    v_ref[...] = (w_exp * xs_ref[...]).astype(BF16)
    d_hi, d_mid, d_lo = _split3(jnp.exp(jnp.concatenate(ends, axis=0)))
    dec = (_dot(jnp.concatenate([d_hi, d_mid], axis=1), e2)
           + _dot(d_lo, e2[:SMALL_W]))

    def state_cols(g):
        return slice(g * SSD_STATE, (g + 1) * SSD_STATE)

    def out_cols(g):
        return slice(SSD_BC + g * SSD_STATE, SSD_BC + (g + 1) * SSD_STATE)

    def val_cols(g):
        return slice(g * SSD_GROUP_W, (g + 1) * SSD_GROUP_W)

    upd = {}

    def issue_updates(c):
        rows = slice(c * CHUNK, (c + 1) * CHUNK)
        for g in range(SSD_GROUPS):
            upd[c, g] = _dot_t0(bc_ref[rows, state_cols(g)], v_ref[rows, val_cols(g)])

    state = [s_ref[g] for g in range(SSD_GROUPS)]
    for c in range(min(_LOOKAHEAD, NCHUNK)):
        issue_updates(c)
    for c in range(NCHUNK):
        rows = slice(c * CHUNK, (c + 1) * CHUNK)
        if c + _LOOKAHEAD < NCHUNK:
            issue_updates(c + _LOOKAHEAD)
        for g in range(SSD_GROUPS):
            vc = val_cols(g)
            state[g] = state[g] * dec[c:c + 1, vc] + upd.pop((c, g))
            y = _dot(bc_ref[rows, out_cols(g)], state[g].astype(BF16))
            y = y + d_ref[:, vc] * xs_ref[rows, vc]
            y = y * _silu(z_ref[rows, vc].astype(F32))
            o_ref[rows, vc] = _rmsnorm(y, nw_ref[:, vc]).astype(BF16)
    for g in range(SSD_GROUPS):
        s_ref[g] = state[g]


def _ssd(p, xs, bc, small, dt_bias, a_log, d_skip, norm_w, tri, bsz, seq):
    t = SEQ_TILE
    nt = seq // t
    pad_l = jnp.zeros((SM_DT,), F32)
    pad_r = jnp.zeros((SMALL_W - SM_DT - SSD_HEADS,), F32)
    dtb = jnp.concatenate([pad_l, dt_bias, pad_r]).reshape(1, SMALL_W)
    alog = jnp.concatenate([pad_l, a_log, pad_r]).reshape(1, SMALL_W)
    head_of_col = jnp.arange(SSD_INNER, dtype=jnp.int32) // SSD_HEADDIM
    expand = (jnp.arange(SMALL_W, dtype=jnp.int32)[:, None] == head_of_col[None, :] + SM_DT).astype(BF16)
    expand = jnp.concatenate([expand, expand], axis=0)
    d_cols = jnp.repeat(d_skip, SSD_HEADDIM).reshape(1, SSD_INNER)

    row = lambda b, j: (b * nt + j, 0)
    return pl.pallas_call(
        _ssd_body,
        grid=(bsz, nt),
        in_specs=[pl.BlockSpec((t, SSD_INNER), lambda b, j: (b * nt + j, P_Z // SSD_INNER)),
                  pl.BlockSpec((t, SSD_INNER), row),
                  pl.BlockSpec((t, 2 * SSD_BC), row),
                  pl.BlockSpec((t, SMALL_W), row),
                  _resident((1, SMALL_W)), _resident((1, SMALL_W)),
                  _resident((2 * SMALL_W, SSD_INNER)), _resident((1, SSD_INNER)),
                  _resident((1, SSD_INNER)), _resident((CHUNK, 2 * CHUNK))],
        out_specs=pl.BlockSpec((t, SSD_INNER), row),
        out_shape=jax.ShapeDtypeStruct((bsz * seq, SSD_INNER), BF16),
        scratch_shapes=[pltpu.VMEM((SSD_GROUPS, SSD_STATE, SSD_GROUP_W), F32),
                        pltpu.VMEM((t, SSD_INNER), BF16)],
        compiler_params=pltpu.CompilerParams(dimension_semantics=("parallel", "arbitrary"),
                                             vmem_limit_bytes=VMEM_LIMIT),
        name="ssd",
    )(p, xs, bc, small, dtb, alog, expand, d_cols, norm_w.reshape(1, SSD_INNER), tri)


def _merge_body(x_ref, oa_ref, yb_ref, gt_ref, wa_ref, wb_ref, wo_ref, o_ref):
    u_a = _dot(oa_ref[...], wa_ref[...])
    u_b = _dot(yb_ref[...], wb_ref[...])
    merged = (jax.nn.sigmoid(gt_ref[:, :D_MODEL].astype(F32)) * u_a
              + jax.nn.sigmoid(gt_ref[:, D_MODEL:].astype(F32)) * u_b)
    o_ref[...] = x_ref[...] + _dot(merged.astype(BF16), wo_ref[...])


def _merge(x2d, o_a, y_b, p, gla_w_o, ssd_w_o, w_out, tm=512):
    m = x2d.shape[0]
    row = lambda i: (i, 0)
    return pl.pallas_call(
        _merge_body,
        grid=(m // tm,),
        in_specs=[pl.BlockSpec((tm, D_MODEL), row),
                  pl.BlockSpec((tm, GLA_VAL), row),
                  pl.BlockSpec((tm, SSD_INNER), row),
                  pl.BlockSpec((tm, 2 * D_MODEL), lambda i: (i, P_GATES // (2 * D_MODEL))),
                  _resident((GLA_VAL, D_MODEL)), _resident((SSD_INNER, D_MODEL)),
                  _resident((D_MODEL, D_MODEL))],
        out_specs=pl.BlockSpec((tm, D_MODEL), row),
        out_shape=jax.ShapeDtypeStruct((m, D_MODEL), F32),
        compiler_params=pltpu.CompilerParams(dimension_semantics=("parallel",),
                                             vmem_limit_bytes=VMEM_LIMIT),
        name="merge",
    )(x2d, o_a, y_b, p, gla_w_o.astype(BF16), ssd_w_o.astype(BF16), w_out.astype(BF16))


def kernel(x, ffn1_norm, ffn1_w_gate, ffn1_w_up, ffn1_w_down, mix_norm, w_in, gla_w_f_up, gla_b_f, gla_norm, gla_w_o, ssd_conv_w, ssd_conv_b, ssd_dt_bias, ssd_a_log, ssd_d, ssd_norm, ssd_w_o, w_out, ffn2_norm, ffn2_w_gate, ffn2_w_up, ffn2_w_down, final_norm):
    bsz, seq, d = x.shape
    depth = ffn1_norm.shape[0]
    assert d == D_MODEL and seq % SEQ_TILE == 0 and depth >= 1
    tri = (jnp.arange(CHUNK)[:, None] >= jnp.arange(CHUNK)[None, :]).astype(BF16)
    tri = jnp.concatenate([tri, tri], axis=1)
    h = x.reshape(bsz * seq, d)
    for l in range(depth):
        last = l == depth - 1
        h = _ffn(h, ffn1_norm[l], ffn1_w_gate[l], ffn1_w_up[l], ffn1_w_down[l])
        p, small, xs, bc = _in_proj(h, mix_norm[l], w_in[l], ssd_conv_w[l], ssd_conv_b[l], seq)
        o_a = _gla(p, small, gla_w_f_up[l], gla_b_f[l], gla_norm[l], tri, bsz, seq)
        y_b = _ssd(p, xs, bc, small, ssd_dt_bias[l], ssd_a_log[l], ssd_d[l], ssd_norm[l], tri,
                   bsz, seq)
        h = _merge(h, o_a, y_b, p, gla_w_o[l], ssd_w_o[l], w_out[l])
        h = _ffn(h, ffn2_norm[l], ffn2_w_gate[l], ffn2_w_up[l], ffn2_w_down[l],
                 final_w=final_norm if last else None)
    return h.reshape(bsz, seq, d)
```

```python
import functools

import jax
import jax.numpy as jnp
from jax import lax
from jax.experimental import pallas as pl
from jax.experimental.pallas import tpu as pltpu

F32 = jnp.float32
BF16 = jnp.bfloat16

D_MODEL = 1024
CHUNK = 64
EPS = 1e-6

GLA_HEADS = 4
GLA_KEY = 512
GLA_VAL = 1024
GLA_DK = 128
GLA_DV = 256
GLA_RANK = 16
GLA_GATE_NORM = 16.0

SSD_INNER = 2048
SSD_HEADDIM = 64
SSD_HEADS = 32
SSD_STATE = 128
SSD_GROUPS = 8
SSD_GROUP_W = SSD_INNER // SSD_GROUPS
SSD_CONV = 4
SSD_CONV_DIM = 4096
SSD_BC = SSD_GROUPS * SSD_STATE

D_FF = 2816

P_Z = 0
P_GATES = 2048
P_V = 4096
P_G = 5120
P_Q = 6144
P_K = 6656
P_WIDTH = 7168
SMALL_W = 128
SM_F = 0
SM_DT = GLA_RANK

VMEM_LIMIT = 56 * 1024 * 1024

SEQ_TILE = 512
NCHUNK = SEQ_TILE // CHUNK
_LOOKAHEAD = 2


def _resident(shape):
    nd = len(shape)
    return pl.BlockSpec(shape, lambda *_: (0,) * nd, pipeline_mode=pl.Buffered(1))


def _rmsnorm(x, w):
    return x * lax.rsqrt(jnp.mean(x * x, axis=-1, keepdims=True) + EPS) * w


def _softplus(x):
    return jnp.maximum(x, 0.0) + jnp.log1p(jnp.exp(-jnp.abs(x)))


def _silu(x):
    return x * jax.nn.sigmoid(x)


def _dot(a, b):
    return jnp.dot(a, b, preferred_element_type=F32)


def _dot_t0(a, b):
    return lax.dot_general(a, b, (((0,), (0,)), ((), ())), preferred_element_type=F32)


def _split2(x):
    hi = x.astype(BF16)
    lo = (x - hi.astype(F32)).astype(BF16)
    return hi, lo


def _split3(x):
    hi = x.astype(BF16)
    r = x - hi.astype(F32)
    mid = r.astype(BF16)
    lo = (r - mid.astype(F32)).astype(BF16)
    return hi, mid, lo


def _cumsum_chunk(tri2, x):
    hi, lo = _split2(x)
    return _dot(tri2, jnp.concatenate([hi, lo], axis=0))


def _ffn_body(x_ref, nw_ref, wgu_ref, wd_ref, *rest, final_norm):
    if final_norm:
        fw_ref, o_ref = rest
    else:
        (o_ref,) = rest
    x = x_ref[...]
    h = _rmsnorm(x, nw_ref[...]).astype(BF16)
    gu = _dot(h, wgu_ref[...])
    g = gu[:, :D_FF]
    u = gu[:, D_FF:]
    act = (_silu(g) * u).astype(BF16)
    out = x + 0.5 * _dot(act, wd_ref[...])
    if final_norm:
        out = _rmsnorm(out, fw_ref[...])
    o_ref[...] = out


def _ffn(x2d, norm_w, w_gate, w_up, w_down, final_w=None, tm=512):
    m = x2d.shape[0]
    wgu = jnp.concatenate([w_gate, w_up], axis=1).astype(BF16)
    wd = w_down.astype(BF16)
    ins = [x2d, norm_w.reshape(1, D_MODEL), wgu, wd]
    specs = [pl.BlockSpec((tm, D_MODEL), lambda i: (i, 0)),
             _resident((1, D_MODEL)), _resident((D_MODEL, 2 * D_FF)), _resident((D_FF, D_MODEL))]
    if final_w is not None:
        ins.append(final_w.reshape(1, D_MODEL))
        specs.append(_resident((1, D_MODEL)))
    return pl.pallas_call(
        functools.partial(_ffn_body, final_norm=final_w is not None),
        grid=(m // tm,),
        in_specs=specs,
        out_specs=pl.BlockSpec((tm, D_MODEL), lambda i: (i, 0)),
        out_shape=jax.ShapeDtypeStruct((m, D_MODEL), F32),
        compiler_params=pltpu.CompilerParams(dimension_semantics=("parallel",),
                                             vmem_limit_bytes=VMEM_LIMIT),
        name="ffn_final" if final_w is not None else "ffn",
    )(*ins)


_PROJ_COL_STEP = 512
_PROJ_TILE = 256
_CONV_COL_STEP = 256
_TAIL = 8


def _inproj_body(x_ref, nw_ref, w_ref, ws_ref, wx_ref, cw_ref, cb_ref,
                 p_ref, s_ref, xs_ref, bc_ref, tail_ref, *, tiles_per_row):
    tm = _PROJ_TILE

    @pl.when(lax.rem(pl.program_id(0), tiles_per_row) == 0)
    def _():
        tail_ref[...] = jnp.zeros_like(tail_ref)

    h = _rmsnorm(x_ref[...], nw_ref[...]).astype(BF16)
    row = lax.broadcasted_iota(jnp.int32, (_TAIL, _CONV_COL_STEP), 0)

    def slab_block(a):
        p_ref[:, a:a + _PROJ_COL_STEP] = _dot(h, w_ref[:, a:a + _PROJ_COL_STEP]).astype(BF16)

    def xbc_block(a):
        cs = slice(a, a + _CONV_COL_STEP)
        r = _dot(h, wx_ref[:, cs])
        tail = tail_ref[:, cs]
        acc = None
        for i in range(SSD_CONV):
            k = SSD_CONV - 1 - i
            if k == 0:
                shifted = r
            else:
                rolled = pltpu.roll(r, k, axis=0)
                head = jnp.where(row < k, pltpu.roll(tail, k, axis=0), rolled[0:_TAIL])
                shifted = jnp.concatenate([head, rolled[_TAIL:]], axis=0)
            term = shifted * cw_ref[i:i + 1, cs]
            acc = term if acc is None else acc + term
        tail_ref[:, cs] = r[tm - _TAIL:tm]
        act = _silu(acc + cb_ref[:, cs])
        if a < SSD_INNER:
            xs_ref[:, cs] = act
        else:
            bc_ref[:, a - SSD_INNER:a - SSD_INNER + _CONV_COL_STEP] = act.astype(BF16)

    plain = [functools.partial(slab_block, a) for a in range(0, P_WIDTH, _PROJ_COL_STEP)]
    fused = [functools.partial(xbc_block, a) for a in range(0, SSD_CONV_DIM, _CONV_COL_STEP)]
    done = 0
    for i, task in enumerate(fused):
        task()
        want = ((i + 1) * len(plain)) // len(fused)
        for j in range(done, want):
            plain[j]()
        done = want
    s_ref[...] = _dot(h, ws_ref[...])


def _in_proj(x2d, norm_w, w_in, conv_w, conv_b, seq):
    m = x2d.shape[0]
    tm = _PROJ_TILE
    o = [0]
    for s in (GLA_KEY, GLA_KEY, GLA_VAL, GLA_VAL, GLA_RANK, SSD_INNER, SSD_CONV_DIM, SSD_HEADS,
              2 * D_MODEL):
        o.append(o[-1] + s)
    wq, wk, wv, wg, wf, wz, wxbc, wdt, wgates = (w_in[:, o[i]:o[i + 1]] for i in range(9))
    w_main = jnp.concatenate([wz, wgates, wv, wg, wq, wk], axis=1).astype(BF16)
    w_small = jnp.concatenate(
        [wf, wdt, jnp.zeros((D_MODEL, SMALL_W - GLA_RANK - SSD_HEADS), F32)], axis=1).astype(BF16)
    row = lambda i: (i, 0)
    return pl.pallas_call(
        functools.partial(_inproj_body, tiles_per_row=seq // tm),
        grid=(m // tm,),
        in_specs=[pl.BlockSpec((tm, D_MODEL), row),
                  _resident((1, D_MODEL)), _resident((D_MODEL, P_WIDTH)),
                  _resident((D_MODEL, SMALL_W)), _resident((D_MODEL, SSD_CONV_DIM)),
                  _resident((SSD_CONV, SSD_CONV_DIM)), _resident((1, SSD_CONV_DIM))],
        out_specs=[pl.BlockSpec((tm, P_WIDTH), row),
                   pl.BlockSpec((tm, SMALL_W), row),
                   pl.BlockSpec((tm, SSD_INNER), row),
                   pl.BlockSpec((tm, 2 * SSD_BC), row)],
        out_shape=[jax.ShapeDtypeStruct((m, P_WIDTH), BF16),
                   jax.ShapeDtypeStruct((m, SMALL_W), F32),
                   jax.ShapeDtypeStruct((m, SSD_INNER), F32),
                   jax.ShapeDtypeStruct((m, 2 * SSD_BC), BF16)],
        scratch_shapes=[pltpu.VMEM((_TAIL, SSD_CONV_DIM), F32)],
        compiler_params=pltpu.CompilerParams(dimension_semantics=("arbitrary",),
                                             vmem_limit_bytes=VMEM_LIMIT),
        name="in_proj_conv",
    )(x2d, norm_w.reshape(1, D_MODEL), w_main, w_small, wxbc.astype(BF16), conv_w,
      conv_b.reshape(1, SSD_CONV_DIM))


_MERGE_PARTS = 2
_PART = SEQ_TILE // _MERGE_PARTS
_PART_CHUNKS = NCHUNK // _MERGE_PARTS
_MERGE_COL_STEP = 256


def _mix_body(q_ref, k_ref, v_ref, g_ref, z_ref, gt_ref, sm_ref, xs_ref, bc_ref, x_ref,
              wf_ref, bf_ref, gnw_ref, dtb_ref, alog_ref, e_ref, d_ref, snw_ref, tri_ref,
              wa_ref, wb_ref, wo_ref,
              o_ref,
              gs_ref, ss_ref, vv_ref, *part_refs):
    @pl.when(pl.program_id(1) == 0)
    def _():
        gs_ref[...] = jnp.zeros_like(gs_ref)
        ss_ref[...] = jnp.zeros_like(ss_ref)

    tri = tri_ref[...]
    oa_refs = part_refs[0:_MERGE_PARTS]
    yb_refs = part_refs[_MERGE_PARTS:2 * _MERGE_PARTS]
    mg_refs = part_refs[2 * _MERGE_PARTS:3 * _MERGE_PARTS]

    def chunk_rows(c):
        return slice(c * CHUNK, (c + 1) * CHUNK)

    def part_rows(c):
        c = c % _PART_CHUNKS
        return slice(c * CHUNK, (c + 1) * CHUNK)

    pre = _dot(sm_ref[...].astype(BF16), wf_ref[...]) + bf_ref[...]
    log_a = -_softplus(-pre) * (1.0 / GLA_GATE_NORM)
    g_ends = []
    kd = []
    for c in range(NCHUNK):
        rows = chunk_rows(c)
        cum = _cumsum_chunk(tri, log_a[rows])
        end = cum[CHUNK - 1:CHUNK]
        g_ends.append(end)
        kd.append((k_ref[rows, :].astype(F32) * jnp.exp(end - cum)).astype(BF16))
    dec_t = jnp.exp(jnp.concatenate(g_ends, axis=0)).T

    lane = lax.broadcasted_iota(jnp.int32, (1, SMALL_W), 1)
    is_dt = (lane >= SM_DT) & (lane < SM_DT + SSD_HEADS)
    dt = jnp.where(is_dt, _softplus(sm_ref[...] + dtb_ref[...]), 0.0)
    a_neg = jnp.where(is_dt, -jnp.exp(alog_ref[...]), 0.0)
    da = dt * a_neg
    e2 = e_ref[...]
    s_ends = []
    wgt = []
    for c in range(NCHUNK):
        rows = chunk_rows(c)
        cum = _cumsum_chunk(tri, da[rows])
        end = cum[CHUNK - 1:CHUNK]
        s_ends.append(end)
        wgt.append(jnp.exp(end - cum) * dt[rows])
    w_hi, w_lo = _split2(jnp.concatenate(wgt, axis=0))
    w_exp = _dot(jnp.concatenate([w_hi, w_lo], axis=1), e2)
    vv_ref[...] = (w_exp * xs_ref[...]).astype(BF16)
    d_hi, d_mid, d_lo = _split3(jnp.exp(jnp.concatenate(s_ends, axis=0)))
    dec = (_dot(jnp.concatenate([d_hi, d_mid], axis=1), e2)
           + _dot(d_lo, e2[:SMALL_W]))

    def key_cols(h):
        return slice(h * GLA_DK, (h + 1) * GLA_DK)

    def gval_cols(h):
        return slice(h * GLA_DV, (h + 1) * GLA_DV)

    def state_cols(g):
        return slice(g * SSD_STATE, (g + 1) * SSD_STATE)

    def out_cols(g):
        return slice(SSD_BC + g * SSD_STATE, SSD_BC + (g + 1) * SSD_STATE)

    def sval_cols(g):
        return slice(g * SSD_GROUP_W, (g + 1) * SSD_GROUP_W)

    g_upd, g_dec, s_upd = {}, {}, {}

    def issue_updates(c):
        rows = chunk_rows(c)
        for h in range(GLA_HEADS):
            g_upd[c, h] = _dot_t0(kd[c][:, key_cols(h)], v_ref[rows, gval_cols(h)])
            g_dec[c, h] = jnp.broadcast_to(dec_t[key_cols(h), c:c + 1], (GLA_DK, GLA_DV))
        for g in range(SSD_GROUPS):
            s_upd[c, g] = _dot_t0(bc_ref[rows, state_cols(g)], vv_ref[rows, sval_cols(g)])

    g_state = [gs_ref[h] for h in range(GLA_HEADS)]
    s_state = [ss_ref[g] for g in range(SSD_GROUPS)]
    gla_eps = EPS * GLA_DK

    def chunk_step(c):
        rows = chunk_rows(c)
        hh, hrows = c // _PART_CHUNKS, part_rows(c)
        for h in range(GLA_HEADS):
            vc = gval_cols(h)
            g_state[h] = g_state[h] * g_dec.pop((c, h)) + g_upd.pop((c, h))
            o = _dot(q_ref[rows, key_cols(h)], g_state[h].astype(BF16))
            o = o * lax.rsqrt(jnp.mean(o * o, axis=-1, keepdims=True) + gla_eps) * gnw_ref[...]
            oa_refs[hh][hrows, vc] = (o * _silu(g_ref[rows, vc].astype(F32))).astype(BF16)
        for g in range(SSD_GROUPS):
            vc = sval_cols(g)
            s_state[g] = s_state[g] * dec[c:c + 1, vc] + s_upd.pop((c, g))
            y = _dot(bc_ref[rows, out_cols(g)], s_state[g].astype(BF16))
            y = y + d_ref[:, vc] * xs_ref[rows, vc]
            y = y * _silu(z_ref[rows, vc].astype(F32))
            yb_refs[hh][hrows, vc] = _rmsnorm(y, snw_ref[:, vc]).astype(BF16)

    def merge_pieces(hh):
        rows = slice(hh * _PART, (hh + 1) * _PART)
        pieces = []
        for j in range(0, D_MODEL, _MERGE_COL_STEP):
            def gate_block(j=j):
                cs = slice(j, j + _MERGE_COL_STEP)
                cs_b = slice(D_MODEL + j, D_MODEL + j + _MERGE_COL_STEP)
                u_a = _dot(oa_refs[hh][...], wa_ref[:, cs])
                u_b = _dot(yb_refs[hh][...], wb_ref[:, cs])
                merged = (jax.nn.sigmoid(gt_ref[rows, cs].astype(F32)) * u_a
                          + jax.nn.sigmoid(gt_ref[rows, cs_b].astype(F32)) * u_b)
                mg_refs[hh][:, cs] = merged.astype(BF16)
            pieces.append(gate_block)
        for j in range(0, D_MODEL, _MERGE_COL_STEP):
            def out_block(j=j):
                cs = slice(j, j + _MERGE_COL_STEP)
                o_ref[rows, cs] = x_ref[rows, cs] + _dot(mg_refs[hh][...], wo_ref[:, cs])
            pieces.append(out_block)
        return pieces

    for c in range(min(_LOOKAHEAD, NCHUNK)):
        issue_updates(c)
    pending = []
    per_chunk = -(-2 * (D_MODEL // _MERGE_COL_STEP) // _PART_CHUNKS)
    for c in range(NCHUNK):
        if c + _LOOKAHEAD < NCHUNK:
            issue_updates(c + _LOOKAHEAD)
        chunk_step(c)
        for piece in pending[:per_chunk]:
            piece()
        pending = pending[per_chunk:]
        if (c + 1) % _PART_CHUNKS == 0:
            pending = pending + merge_pieces(c // _PART_CHUNKS)
    for piece in pending:
        piece()
    for h in range(GLA_HEADS):
        gs_ref[h] = g_state[h]
    for g in range(SSD_GROUPS):
        ss_ref[g] = s_state[g]


def _mix(x2d, p, small, xs, bc, w_f_up, b_f, gla_norm, dt_bias, a_log, d_skip, ssd_norm,
         gla_w_o, ssd_w_o, w_out, bsz, seq):
    t = SEQ_TILE
    nt = seq // t
    tri = (jnp.arange(CHUNK)[:, None] >= jnp.arange(CHUNK)[None, :]).astype(BF16)
    tri = jnp.concatenate([tri, tri], axis=1)
    wf = jnp.concatenate([w_f_up, jnp.zeros((SMALL_W - GLA_RANK, GLA_KEY), F32)], axis=0).astype(BF16)
    pad_l = jnp.zeros((SM_DT,), F32)
    pad_r = jnp.zeros((SMALL_W - SM_DT - SSD_HEADS,), F32)
    dtb = jnp.concatenate([pad_l, dt_bias, pad_r]).reshape(1, SMALL_W)
    alog = jnp.concatenate([pad_l, a_log, pad_r]).reshape(1, SMALL_W)
    head_of_col = jnp.arange(SSD_INNER, dtype=jnp.int32) // SSD_HEADDIM
    expand = (jnp.arange(SMALL_W, dtype=jnp.int32)[:, None] == head_of_col[None, :] + SM_DT).astype(BF16)
    expand = jnp.concatenate([expand, expand], axis=0)
    d_cols = jnp.repeat(d_skip, SSD_HEADDIM).reshape(1, SSD_INNER)

    def col(width, off):
        blk = off // width
        return pl.BlockSpec((t, width), lambda b, j: (b * nt + j, blk))

    row = lambda b, j: (b * nt + j, 0)
    return pl.pallas_call(
        _mix_body,
        grid=(bsz, nt),
        in_specs=[col(GLA_KEY, P_Q), col(GLA_KEY, P_K), col(GLA_VAL, P_V), col(GLA_VAL, P_G),
                  col(SSD_INNER, P_Z), col(2 * D_MODEL, P_GATES),
                  pl.BlockSpec((t, SMALL_W), row),
                  pl.BlockSpec((t, SSD_INNER), row),
                  pl.BlockSpec((t, 2 * SSD_BC), row),
                  pl.BlockSpec((t, D_MODEL), row),
                  _resident((SMALL_W, GLA_KEY)), _resident((1, GLA_KEY)), _resident((1, GLA_DV)),
                  _resident((1, SMALL_W)), _resident((1, SMALL_W)),
                  _resident((2 * SMALL_W, SSD_INNER)), _resident((1, SSD_INNER)),
                  _resident((1, SSD_INNER)), _resident((CHUNK, 2 * CHUNK)),
                  _resident((GLA_VAL, D_MODEL)), _resident((SSD_INNER, D_MODEL)),
                  _resident((D_MODEL, D_MODEL))],
        out_specs=pl.BlockSpec((t, D_MODEL), row),
        out_shape=jax.ShapeDtypeStruct((bsz * seq, D_MODEL), F32),
        scratch_shapes=[pltpu.VMEM((GLA_HEADS, GLA_DK, GLA_DV), F32),
                        pltpu.VMEM((SSD_GROUPS, SSD_STATE, SSD_GROUP_W), F32),
                        pltpu.VMEM((t, SSD_INNER), BF16),
                        *([pltpu.VMEM((_PART, GLA_VAL), BF16)] * _MERGE_PARTS),
                        *([pltpu.VMEM((_PART, SSD_INNER), BF16)] * _MERGE_PARTS),
                        *([pltpu.VMEM((_PART, D_MODEL), BF16)] * _MERGE_PARTS)],
        compiler_params=pltpu.CompilerParams(dimension_semantics=("parallel", "arbitrary"),
                                             vmem_limit_bytes=VMEM_LIMIT),
        name="mix",
    )(p, p, p, p, p, p, small, xs, bc, x2d, wf, b_f.reshape(1, GLA_KEY),
      gla_norm.reshape(1, GLA_DV), dtb, alog, expand, d_cols, ssd_norm.reshape(1, SSD_INNER), tri,
      gla_w_o.astype(BF16), ssd_w_o.astype(BF16), w_out.astype(BF16))


def kernel(x, ffn1_norm, ffn1_w_gate, ffn1_w_up, ffn1_w_down, mix_norm, w_in, gla_w_f_up, gla_b_f, gla_norm, gla_w_o, ssd_conv_w, ssd_conv_b, ssd_dt_bias, ssd_a_log, ssd_d, ssd_norm, ssd_w_o, w_out, ffn2_norm, ffn2_w_gate, ffn2_w_up, ffn2_w_down, final_norm):
    bsz, seq, d = x.shape
    depth = ffn1_norm.shape[0]
    assert d == D_MODEL and seq % SEQ_TILE == 0 and depth >= 1
    h = x.reshape(bsz * seq, d)
    for l in range(depth):
        last = l == depth - 1
        h = _ffn(h, ffn1_norm[l], ffn1_w_gate[l], ffn1_w_up[l], ffn1_w_down[l])
        p, small, xs, bc = _in_proj(h, mix_norm[l], w_in[l], ssd_conv_w[l], ssd_conv_b[l], seq)
        h = _mix(h, p, small, xs, bc, gla_w_f_up[l], gla_b_f[l], gla_norm[l], ssd_dt_bias[l],
                 ssd_a_log[l], ssd_d[l], ssd_norm[l], gla_w_o[l], ssd_w_o[l], w_out[l], bsz, seq)
        h = _ffn(h, ffn2_norm[l], ffn2_w_gate[l], ffn2_w_up[l], ffn2_w_down[l],
                 final_w=final_norm if last else None)
    return h.reshape(bsz, seq, d)
```

```python
import functools

import jax
import jax.numpy as jnp
from jax import lax
from jax.experimental import pallas as pl
from jax.experimental.pallas import tpu as pltpu

F32 = jnp.float32
BF16 = jnp.bfloat16

D_MODEL = 1024
CHUNK = 64
EPS = 1e-6

GLA_HEADS = 4
GLA_KEY = 512
GLA_VAL = 1024
GLA_DK = 128
GLA_DV = 256
GLA_RANK = 16
GLA_GATE_NORM = 16.0

SSD_INNER = 2048
SSD_HEADDIM = 64
SSD_HEADS = 32
SSD_STATE = 128
SSD_GROUPS = 8
SSD_GROUP_W = SSD_INNER // SSD_GROUPS
SSD_CONV = 4
SSD_CONV_DIM = 4096
SSD_BC = SSD_GROUPS * SSD_STATE

D_FF = 2816

P_Z = 0
P_GATES = 2048
P_V = 4096
P_G = 5120
P_Q = 6144
P_K = 6656
P_WIDTH = 7168
SMALL_W = 128
SM_F = 0
SM_DT = GLA_RANK

VMEM_LIMIT = 56 * 1024 * 1024

SEQ_TILE = 512
NCHUNK = SEQ_TILE // CHUNK
_LOOKAHEAD = 2


def _resident(shape):
    nd = len(shape)
    return pl.BlockSpec(shape, lambda *_: (0,) * nd, pipeline_mode=pl.Buffered(1))


def _rmsnorm(x, w):
    return x * lax.rsqrt(jnp.mean(x * x, axis=-1, keepdims=True) + EPS) * w


def _softplus(x):
    return jnp.maximum(x, 0.0) + jnp.log1p(jnp.exp(-jnp.abs(x)))


def _log_sigmoid(x):
    return jnp.minimum(x, 0.0) - jnp.log(1.0 + jnp.exp(-jnp.abs(x)))


def _silu(x):
    return x * jax.nn.sigmoid(x)


def _silu_of_twice(hx):
    return hx + hx * jnp.tanh(hx)


def _dot(a, b):
    return jnp.dot(a, b, preferred_element_type=F32)


def _dot_t0(a, b):
    return lax.dot_general(a, b, (((0,), (0,)), ((), ())), preferred_element_type=F32)


def _split2(x):
    hi = x.astype(BF16)
    lo = (x - hi.astype(F32)).astype(BF16)
    return hi, lo


def _split3(x):
    hi = x.astype(BF16)
    r = x - hi.astype(F32)
    mid = r.astype(BF16)
    lo = (r - mid.astype(F32)).astype(BF16)
    return hi, mid, lo


def _cumsum_chunk(tri2, x):
    hi, lo = _split2(x)
    return _dot(tri2, jnp.concatenate([hi, lo], axis=0))


def _ffn_body(x_ref, nw_ref, wgu_ref, wd_ref, *rest, final_norm):
    if final_norm:
        fw_ref, o_ref = rest
    else:
        (o_ref,) = rest
    x = x_ref[...]
    h = _rmsnorm(x, nw_ref[...]).astype(BF16)
    gu = _dot(h, wgu_ref[...])
    g = gu[:, :D_FF]
    u = gu[:, D_FF:]
    act = (_silu(g) * u).astype(BF16)
    out = x + 0.5 * _dot(act, wd_ref[...])
    if final_norm:
        out = _rmsnorm(out, fw_ref[...])
    o_ref[...] = out


def _ffn(x2d, norm_w, w_gate, w_up, w_down, final_w=None, tm=512):
    m = x2d.shape[0]
    wgu = jnp.concatenate([w_gate, w_up], axis=1).astype(BF16)
    wd = w_down.astype(BF16)
    ins = [x2d, norm_w.reshape(1, D_MODEL), wgu, wd]
    specs = [pl.BlockSpec((tm, D_MODEL), lambda i: (i, 0)),
             _resident((1, D_MODEL)), _resident((D_MODEL, 2 * D_FF)), _resident((D_FF, D_MODEL))]
    if final_w is not None:
        ins.append(final_w.reshape(1, D_MODEL))
        specs.append(_resident((1, D_MODEL)))
    return pl.pallas_call(
        functools.partial(_ffn_body, final_norm=final_w is not None),
        grid=(m // tm,),
        in_specs=specs,
        out_specs=pl.BlockSpec((tm, D_MODEL), lambda i: (i, 0)),
        out_shape=jax.ShapeDtypeStruct((m, D_MODEL), F32),
        compiler_params=pltpu.CompilerParams(dimension_semantics=("parallel",),
                                             vmem_limit_bytes=VMEM_LIMIT),
        name="ffn_final" if final_w is not None else "ffn",
    )(*ins)


_PROJ_COL_STEP = 512
_PROJ_TILE = 256
_CONV_COL_STEP = 256
_TAIL = 8


def _inproj_body(x_ref, nw_ref, w_ref, ws_ref, wx_ref, cw_ref, cb_ref,
                 p_ref, s_ref, xs_ref, bc_ref, tail_ref, *, tiles_per_row):
    tm = _PROJ_TILE

    @pl.when(lax.rem(pl.program_id(0), tiles_per_row) == 0)
    def _():
        tail_ref[...] = jnp.zeros_like(tail_ref)

    h = _rmsnorm(x_ref[...], nw_ref[...]).astype(BF16)
    row = lax.broadcasted_iota(jnp.int32, (_TAIL, _CONV_COL_STEP), 0)

    def slab_block(a):
        p_ref[:, a:a + _PROJ_COL_STEP] = _dot(h, w_ref[:, a:a + _PROJ_COL_STEP]).astype(BF16)

    def xbc_block(a):
        cs = slice(a, a + _CONV_COL_STEP)
        r = _dot(h, wx_ref[:, cs])
        tail = tail_ref[:, cs]
        acc = None
        for i in range(SSD_CONV):
            k = SSD_CONV - 1 - i
            if k == 0:
                shifted = r
            else:
                rolled = pltpu.roll(r, k, axis=0)
                head = jnp.where(row < k, pltpu.roll(tail, k, axis=0), rolled[0:_TAIL])
                shifted = jnp.concatenate([head, rolled[_TAIL:]], axis=0)
            term = shifted * cw_ref[i:i + 1, cs]
            acc = term if acc is None else acc + term
        tail_ref[:, cs] = r[tm - _TAIL:tm]
        act = _silu_of_twice(acc + cb_ref[:, cs])
        if a < SSD_INNER:
            xs_ref[:, cs] = act
        else:
            bc_ref[:, a - SSD_INNER:a - SSD_INNER + _CONV_COL_STEP] = act.astype(BF16)

    plain = [functools.partial(slab_block, a) for a in range(0, P_WIDTH, _PROJ_COL_STEP)]
    fused = [functools.partial(xbc_block, a) for a in range(0, SSD_CONV_DIM, _CONV_COL_STEP)]
    done = 0
    for i, task in enumerate(fused):
        task()
        want = ((i + 1) * len(plain)) // len(fused)
        for j in range(done, want):
            plain[j]()
        done = want
    s_ref[...] = _dot(h, ws_ref[...])


def _in_proj(x2d, norm_w, w_in, conv_w, conv_b, seq):
    m = x2d.shape[0]
    tm = _PROJ_TILE
    o = [0]
    for s in (GLA_KEY, GLA_KEY, GLA_VAL, GLA_VAL, GLA_RANK, SSD_INNER, SSD_CONV_DIM, SSD_HEADS,
              2 * D_MODEL):
        o.append(o[-1] + s)
    wq, wk, wv, wg, wf, wz, wxbc, wdt, wgates = (w_in[:, o[i]:o[i + 1]] for i in range(9))
    w_main = jnp.concatenate([0.5 * wz, 0.5 * wgates, wv, 0.5 * wg, wq, wk], axis=1).astype(BF16)
    w_small = jnp.concatenate(
        [wf, wdt, jnp.zeros((D_MODEL, SMALL_W - GLA_RANK - SSD_HEADS), F32)], axis=1).astype(BF16)
    row = lambda i: (i, 0)
    return pl.pallas_call(
        functools.partial(_inproj_body, tiles_per_row=seq // tm),
        grid=(m // tm,),
        in_specs=[pl.BlockSpec((tm, D_MODEL), row),
                  _resident((1, D_MODEL)), _resident((D_MODEL, P_WIDTH)),
                  _resident((D_MODEL, SMALL_W)), _resident((D_MODEL, SSD_CONV_DIM)),
                  _resident((SSD_CONV, SSD_CONV_DIM)), _resident((1, SSD_CONV_DIM))],
        out_specs=[pl.BlockSpec((tm, P_WIDTH), row),
                   pl.BlockSpec((tm, SMALL_W), row),
                   pl.BlockSpec((tm, SSD_INNER), row),
                   pl.BlockSpec((tm, 2 * SSD_BC), row)],
        out_shape=[jax.ShapeDtypeStruct((m, P_WIDTH), BF16),
                   jax.ShapeDtypeStruct((m, SMALL_W), F32),
                   jax.ShapeDtypeStruct((m, SSD_INNER), F32),
                   jax.ShapeDtypeStruct((m, 2 * SSD_BC), BF16)],
        scratch_shapes=[pltpu.VMEM((_TAIL, SSD_CONV_DIM), F32)],
        compiler_params=pltpu.CompilerParams(dimension_semantics=("arbitrary",),
                                             vmem_limit_bytes=VMEM_LIMIT),
        name="in_proj_conv",
    )(x2d, norm_w.reshape(1, D_MODEL), w_main, w_small, wxbc.astype(BF16), 0.5 * conv_w,
      0.5 * conv_b.reshape(1, SSD_CONV_DIM))


_MERGE_PARTS = 2
_PART = SEQ_TILE // _MERGE_PARTS
_PART_CHUNKS = NCHUNK // _MERGE_PARTS
_MERGE_COL_STEP = 256


def _mix_body(q_ref, k_ref, v_ref, g_ref, z_ref, gt_ref, sm_ref, xs_ref, bc_ref, x_ref,
              wf_ref, bf_ref, dtb_ref, alog_ref, e_ref, d_ref, tri_ref,
              wa_ref, wb_ref, wo_ref,
              o_ref,
              gs_ref, ss_ref, vv_ref, *part_refs):
    @pl.when(pl.program_id(1) == 0)
    def _():
        gs_ref[...] = jnp.zeros_like(gs_ref)
        ss_ref[...] = jnp.zeros_like(ss_ref)

    tri = tri_ref[...]
    oa_refs = part_refs[0:_MERGE_PARTS]
    yb_refs = part_refs[_MERGE_PARTS:2 * _MERGE_PARTS]
    mg_refs = part_refs[2 * _MERGE_PARTS:3 * _MERGE_PARTS]

    def chunk_rows(c):
        return slice(c * CHUNK, (c + 1) * CHUNK)

    def part_rows(c):
        c = c % _PART_CHUNKS
        return slice(c * CHUNK, (c + 1) * CHUNK)

    pre = _dot(sm_ref[...].astype(BF16), wf_ref[...]) + bf_ref[...]
    log_a = _log_sigmoid(pre) * (1.0 / GLA_GATE_NORM)
    g_ends = []
    kd = []
    for c in range(NCHUNK):
        rows = chunk_rows(c)
        cum = _cumsum_chunk(tri, log_a[rows])
        end = cum[CHUNK - 1:CHUNK]
        g_ends.append(end)
        kd.append((k_ref[rows, :].astype(F32) * jnp.exp(end - cum)).astype(BF16))
    dec_t = jnp.exp(jnp.concatenate(g_ends, axis=0)).T

    lane = lax.broadcasted_iota(jnp.int32, (1, SMALL_W), 1)
    is_dt = (lane >= SM_DT) & (lane < SM_DT + SSD_HEADS)
    dt = jnp.where(is_dt, _softplus(sm_ref[...] + dtb_ref[...]), 0.0)
    a_neg = jnp.where(is_dt, -jnp.exp(alog_ref[...]), 0.0)
    da = dt * a_neg
    e2 = e_ref[...]
    s_ends = []
    wgt = []
    for c in range(NCHUNK):
        rows = chunk_rows(c)
        cum = _cumsum_chunk(tri, da[rows])
        end = cum[CHUNK - 1:CHUNK]
        s_ends.append(end)
        wgt.append(jnp.exp(end - cum) * dt[rows])
    w_hi, w_lo = _split2(jnp.concatenate(wgt, axis=0))
    w_exp = _dot(jnp.concatenate([w_hi, w_lo], axis=1), e2)
    vv_ref[...] = (w_exp * xs_ref[...]).astype(BF16)
    d_hi, d_mid, d_lo = _split3(jnp.exp(jnp.concatenate(s_ends, axis=0)))
    dec = (_dot(jnp.concatenate([d_hi, d_mid], axis=1), e2)
           + _dot(d_lo, e2[:SMALL_W]))

    def key_cols(h):
        return slice(h * GLA_DK, (h + 1) * GLA_DK)

    def gval_cols(h):
        return slice(h * GLA_DV, (h + 1) * GLA_DV)

    def state_cols(g):
        return slice(g * SSD_STATE, (g + 1) * SSD_STATE)

    def out_cols(g):
        return slice(SSD_BC + g * SSD_STATE, SSD_BC + (g + 1) * SSD_STATE)

    def sval_cols(g):
        return slice(g * SSD_GROUP_W, (g + 1) * SSD_GROUP_W)

    g_upd, g_dec, s_upd = {}, {}, {}

    def issue_updates(c):
        rows = chunk_rows(c)
        for h in range(GLA_HEADS):
            g_upd[c, h] = _dot_t0(kd[c][:, key_cols(h)], v_ref[rows, gval_cols(h)])
            g_dec[c, h] = jnp.broadcast_to(dec_t[key_cols(h), c:c + 1], (GLA_DK, GLA_DV))
        for g in range(SSD_GROUPS):
            s_upd[c, g] = _dot_t0(bc_ref[rows, state_cols(g)], vv_ref[rows, sval_cols(g)])

    g_state = [gs_ref[h] for h in range(GLA_HEADS)]
    s_state = [ss_ref[g] for g in range(SSD_GROUPS)]
    gla_eps = EPS * GLA_DK

    def chunk_step(c):
        rows = chunk_rows(c)
        hh, hrows = c // _PART_CHUNKS, part_rows(c)
        for h in range(GLA_HEADS):
            vc = gval_cols(h)
            g_state[h] = g_state[h] * g_dec.pop((c, h)) + g_upd.pop((c, h))
            o = _dot(q_ref[rows, key_cols(h)], g_state[h].astype(BF16))
            o = o * lax.rsqrt(jnp.mean(o * o, axis=-1, keepdims=True) + gla_eps)
            oa_refs[hh][hrows, vc] = (o * _silu_of_twice(g_ref[rows, vc].astype(F32))).astype(BF16)
        for g in range(SSD_GROUPS):
            vc = sval_cols(g)
            s_state[g] = s_state[g] * dec[c:c + 1, vc] + s_upd.pop((c, g))
            y = _dot(bc_ref[rows, out_cols(g)], s_state[g].astype(BF16))
            y = y + d_ref[:, vc] * xs_ref[rows, vc]
            y = y * _silu_of_twice(z_ref[rows, vc].astype(F32))
            y = y * lax.rsqrt(jnp.mean(y * y, axis=-1, keepdims=True) + EPS)
            yb_refs[hh][hrows, vc] = y.astype(BF16)

    def merge_pieces(hh):
        rows = slice(hh * _PART, (hh + 1) * _PART)
        pieces = []
        for j in range(0, D_MODEL, _MERGE_COL_STEP):
            def gate_block(j=j):
                cs = slice(j, j + _MERGE_COL_STEP)
                cs_b = slice(D_MODEL + j, D_MODEL + j + _MERGE_COL_STEP)
                u_a = _dot(oa_refs[hh][...], wa_ref[:, cs])
                u_b = _dot(yb_refs[hh][...], wb_ref[:, cs])
                merged = ((u_a + jnp.tanh(gt_ref[rows, cs].astype(F32)) * u_a)
                          + (u_b + jnp.tanh(gt_ref[rows, cs_b].astype(F32)) * u_b))
                mg_refs[hh][:, cs] = merged.astype(BF16)
            pieces.append(gate_block)
        for j in range(0, D_MODEL, _MERGE_COL_STEP):
            def out_block(j=j):
                cs = slice(j, j + _MERGE_COL_STEP)
                o_ref[rows, cs] = x_ref[rows, cs] + _dot(mg_refs[hh][...], wo_ref[:, cs])
            pieces.append(out_block)
        return pieces

    for c in range(min(_LOOKAHEAD, NCHUNK)):
        issue_updates(c)
    pending = []
    per_chunk = -(-2 * (D_MODEL // _MERGE_COL_STEP) // _PART_CHUNKS)
    for c in range(NCHUNK):
        if c + _LOOKAHEAD < NCHUNK:
            issue_updates(c + _LOOKAHEAD)
        chunk_step(c)
        for piece in pending[:per_chunk]:
            piece()
        pending = pending[per_chunk:]
        if (c + 1) % _PART_CHUNKS == 0:
            pending = pending + merge_pieces(c // _PART_CHUNKS)
    for piece in pending:
        piece()
    for h in range(GLA_HEADS):
        gs_ref[h] = g_state[h]
    for g in range(SSD_GROUPS):
        ss_ref[g] = s_state[g]


def _mix(x2d, p, small, xs, bc, w_f_up, b_f, gla_norm, dt_bias, a_log, d_skip, ssd_norm,
         gla_w_o, ssd_w_o, w_out, bsz, seq):
    t = SEQ_TILE
    nt = seq // t
    tri = (jnp.arange(CHUNK)[:, None] >= jnp.arange(CHUNK)[None, :]).astype(BF16)
    tri = jnp.concatenate([tri, tri], axis=1)
    wf = jnp.concatenate([w_f_up, jnp.zeros((SMALL_W - GLA_RANK, GLA_KEY), F32)], axis=0).astype(BF16)
    pad_l = jnp.zeros((SM_DT,), F32)
    pad_r = jnp.zeros((SMALL_W - SM_DT - SSD_HEADS,), F32)
    dtb = jnp.concatenate([pad_l, dt_bias, pad_r]).reshape(1, SMALL_W)
    alog = jnp.concatenate([pad_l, a_log, pad_r]).reshape(1, SMALL_W)
    head_of_col = jnp.arange(SSD_INNER, dtype=jnp.int32) // SSD_HEADDIM
    expand = (jnp.arange(SMALL_W, dtype=jnp.int32)[:, None] == head_of_col[None, :] + SM_DT).astype(BF16)
    expand = jnp.concatenate([expand, expand], axis=0)
    d_cols = jnp.repeat(d_skip, SSD_HEADDIM).reshape(1, SSD_INNER)
    wa = (jnp.tile(gla_norm, GLA_HEADS)[:, None] * gla_w_o).astype(BF16)
    wb = (ssd_norm[:, None] * ssd_w_o).astype(BF16)
    wo = (0.5 * w_out).astype(BF16)

    def col(width, off):
        blk = off // width
        return pl.BlockSpec((t, width), lambda b, j: (b * nt + j, blk))

    row = lambda b, j: (b * nt + j, 0)
    return pl.pallas_call(
        _mix_body,
        grid=(bsz, nt),
        in_specs=[col(GLA_KEY, P_Q), col(GLA_KEY, P_K), col(GLA_VAL, P_V), col(GLA_VAL, P_G),
                  col(SSD_INNER, P_Z), col(2 * D_MODEL, P_GATES),
                  pl.BlockSpec((t, SMALL_W), row),
                  pl.BlockSpec((t, SSD_INNER), row),
                  pl.BlockSpec((t, 2 * SSD_BC), row),
                  pl.BlockSpec((t, D_MODEL), row),
                  _resident((SMALL_W, GLA_KEY)), _resident((1, GLA_KEY)),
                  _resident((1, SMALL_W)), _resident((1, SMALL_W)),
                  _resident((2 * SMALL_W, SSD_INNER)), _resident((1, SSD_INNER)),
                  _resident((CHUNK, 2 * CHUNK)),
                  _resident((GLA_VAL, D_MODEL)), _resident((SSD_INNER, D_MODEL)),
                  _resident((D_MODEL, D_MODEL))],
        out_specs=pl.BlockSpec((t, D_MODEL), row),
        out_shape=jax.ShapeDtypeStruct((bsz * seq, D_MODEL), F32),
        scratch_shapes=[pltpu.VMEM((GLA_HEADS, GLA_DK, GLA_DV), F32),
                        pltpu.VMEM((SSD_GROUPS, SSD_STATE, SSD_GROUP_W), F32),
                        pltpu.VMEM((t, SSD_INNER), BF16),
                        *([pltpu.VMEM((_PART, GLA_VAL), BF16)] * _MERGE_PARTS),
                        *([pltpu.VMEM((_PART, SSD_INNER), BF16)] * _MERGE_PARTS),
                        *([pltpu.VMEM((_PART, D_MODEL), BF16)] * _MERGE_PARTS)],
        compiler_params=pltpu.CompilerParams(dimension_semantics=("parallel", "arbitrary"),
                                             vmem_limit_bytes=VMEM_LIMIT),
        name="mix",
    )(p, p, p, p, p, p, small, xs, bc, x2d, wf, b_f.reshape(1, GLA_KEY),
      dtb, alog, expand, d_cols, tri, wa, wb, wo)


def kernel(x, ffn1_norm, ffn1_w_gate, ffn1_w_up, ffn1_w_down, mix_norm, w_in, gla_w_f_up, gla_b_f, gla_norm, gla_w_o, ssd_conv_w, ssd_conv_b, ssd_dt_bias, ssd_a_log, ssd_d, ssd_norm, ssd_w_o, w_out, ffn2_norm, ffn2_w_gate, ffn2_w_up, ffn2_w_down, final_norm):
    bsz, seq, d = x.shape
    depth = ffn1_norm.shape[0]
    assert d == D_MODEL and seq % SEQ_TILE == 0 and depth >= 1
    h = x.reshape(bsz * seq, d)
    for l in range(depth):
        last = l == depth - 1
        h = _ffn(h, ffn1_norm[l], ffn1_w_gate[l], ffn1_w_up[l], ffn1_w_down[l])
        p, small, xs, bc = _in_proj(h, mix_norm[l], w_in[l], ssd_conv_w[l], ssd_conv_b[l], seq)
        h = _mix(h, p, small, xs, bc, gla_w_f_up[l], gla_b_f[l], gla_norm[l], ssd_dt_bias[l],
                 ssd_a_log[l], ssd_d[l], ssd_norm[l], gla_w_o[l], ssd_w_o[l], w_out[l], bsz, seq)
        h = _ffn(h, ffn2_norm[l], ffn2_w_gate[l], ffn2_w_up[l], ffn2_w_down[l],
                 final_w=final_norm if last else None)
    return h.reshape(bsz, seq, d)
```

```python
import functools

import jax
import jax.numpy as jnp
from jax import lax
from jax.experimental import pallas as pl
from jax.experimental.pallas import tpu as pltpu

F32 = jnp.float32
BF16 = jnp.bfloat16

D_MODEL = 1024
CHUNK = 64
EPS = 1e-6

GLA_HEADS = 4
GLA_KEY = 512
GLA_VAL = 1024
GLA_DK = 128
GLA_DV = 256
GLA_RANK = 16
GLA_GATE_NORM = 16.0

SSD_INNER = 2048
SSD_HEADDIM = 64
SSD_HEADS = 32
SSD_STATE = 128
SSD_GROUPS = 8
SSD_GROUP_W = SSD_INNER // SSD_GROUPS
SSD_CONV = 4
SSD_CONV_DIM = 4096
SSD_BC = SSD_GROUPS * SSD_STATE

D_FF = 2816

P_Z = 0
P_GATES = 2048
P_V = 4096
P_G = 5120
P_Q = 6144
P_K = 6656
P_WIDTH = 7168
SMALL_W = 128
SM_F = 0
SM_DT = GLA_RANK

VMEM_LIMIT = 56 * 1024 * 1024

SEQ_TILE = 512
NCHUNK = SEQ_TILE // CHUNK
_LOOKAHEAD = 2


def _resident(shape):
    nd = len(shape)
    return pl.BlockSpec(shape, lambda *_: (0,) * nd, pipeline_mode=pl.Buffered(1))


def _rmsnorm(x, w):
    return x * lax.rsqrt(jnp.mean(x * x, axis=-1, keepdims=True) + EPS) * w


def _softplus(x):
    return jnp.maximum(x, 0.0) + jnp.log1p(jnp.exp(-jnp.abs(x)))


def _log_sigmoid(x):
    return jnp.minimum(x, 0.0) - jnp.log(1.0 + jnp.exp(-jnp.abs(x)))


def _silu_of_twice(hx):
    return hx + hx * jnp.tanh(hx)


def _dot(a, b):
    return jnp.dot(a, b, preferred_element_type=F32)


def _dot_t0(a, b):
    return lax.dot_general(a, b, (((0,), (0,)), ((), ())), preferred_element_type=F32)


def _split2(x):
    hi = x.astype(BF16)
    lo = (x - hi.astype(F32)).astype(BF16)
    return hi, lo


def _split3(x):
    hi = x.astype(BF16)
    r = x - hi.astype(F32)
    mid = r.astype(BF16)
    lo = (r - mid.astype(F32)).astype(BF16)
    return hi, mid, lo


def _cumsum_chunk(tri2, x):
    hi, lo = _split2(x)
    return _dot(tri2, jnp.concatenate([hi, lo], axis=0))


def _ffn_body(x_ref, nw_ref, wg_ref, wu_ref, wd_ref, *rest, final_norm):
    if final_norm:
        fw_ref, o_ref = rest
    else:
        (o_ref,) = rest
    x = x_ref[...]
    h = _rmsnorm(x, nw_ref[...]).astype(BF16)
    g = _dot(h, wg_ref[...])
    u = _dot(h, wu_ref[...])
    act = (_silu_of_twice(g) * u).astype(BF16)
    out = x + 0.5 * _dot(act, wd_ref[...])
    if final_norm:
        out = _rmsnorm(out, fw_ref[...])
    o_ref[...] = out


def _ffn(x2d, norm_w, w_gate, w_up, w_down, final_w=None, tm=512):
    m = x2d.shape[0]
    ins = [x2d, norm_w.reshape(1, D_MODEL), (0.5 * w_gate).astype(BF16), w_up.astype(BF16),
           w_down.astype(BF16)]
    specs = [pl.BlockSpec((tm, D_MODEL), lambda i: (i, 0)),
             _resident((1, D_MODEL)), _resident((D_MODEL, D_FF)), _resident((D_MODEL, D_FF)),
             _resident((D_FF, D_MODEL))]
    if final_w is not None:
        ins.append(final_w.reshape(1, D_MODEL))
        specs.append(_resident((1, D_MODEL)))
    return pl.pallas_call(
        functools.partial(_ffn_body, final_norm=final_w is not None),
        grid=(m // tm,),
        in_specs=specs,
        out_specs=pl.BlockSpec((tm, D_MODEL), lambda i: (i, 0)),
        out_shape=jax.ShapeDtypeStruct((m, D_MODEL), F32),
        compiler_params=pltpu.CompilerParams(dimension_semantics=("parallel",),
                                             vmem_limit_bytes=VMEM_LIMIT),
        name="ffn_final" if final_w is not None else "ffn",
    )(*ins)


_PROJ_COL_STEP = 512
_PROJ_TILE = 256
_CONV_COL_STEP = 256
_TAIL = 8


def _inproj_body(x_ref, nw_ref, w_ref, ws_ref, wx_ref, cw_ref, cb_ref,
                 p_ref, s_ref, xs_ref, bc_ref, tail_ref, *, tiles_per_row):
    tm = _PROJ_TILE

    @pl.when(lax.rem(pl.program_id(0), tiles_per_row) == 0)
    def _():
        tail_ref[...] = jnp.zeros_like(tail_ref)

    h = _rmsnorm(x_ref[...], nw_ref[...]).astype(BF16)
    row = lax.broadcasted_iota(jnp.int32, (_TAIL, _CONV_COL_STEP), 0)

    def slab_block(a):
        p_ref[:, a:a + _PROJ_COL_STEP] = _dot(h, w_ref[:, a:a + _PROJ_COL_STEP]).astype(BF16)

    def xbc_block(a):
        cs = slice(a, a + _CONV_COL_STEP)
        r = _dot(h, wx_ref[:, cs])
        tail = tail_ref[:, cs]
        w0, w1, w2, w3 = (cw_ref[i:i + 1, cs] for i in range(SSD_CONV))

        def delay(cur, prev, k):
            rolled = pltpu.roll(cur, k, axis=0)
            head = jnp.where(row < k, pltpu.roll(prev, k, axis=0), rolled[0:_TAIL])
            return jnp.concatenate([head, rolled[_TAIL:]], axis=0)

        r1 = delay(r, tail, 1)
        t1 = pltpu.roll(tail, 1, axis=0)
        b = r * w1 + r1 * w0
        b_tail = tail * w1 + t1 * w0
        acc = (r * w3 + r1 * w2) + delay(b, b_tail, 2)
        tail_ref[:, cs] = r[tm - _TAIL:tm]
        act = _silu_of_twice(acc + cb_ref[:, cs])
        if a < SSD_INNER:
            xs_ref[:, cs] = act
        else:
            bc_ref[:, a - SSD_INNER:a - SSD_INNER + _CONV_COL_STEP] = act.astype(BF16)

    plain = [functools.partial(slab_block, a) for a in range(0, P_WIDTH, _PROJ_COL_STEP)]
    fused = [functools.partial(xbc_block, a) for a in range(0, SSD_CONV_DIM, _CONV_COL_STEP)]
    done = 0
    for i, task in enumerate(fused):
        task()
        want = ((i + 1) * len(plain)) // len(fused)
        for j in range(done, want):
            plain[j]()
        done = want
    s_ref[...] = _dot(h, ws_ref[...])


def _in_proj(x2d, norm_w, w_in, conv_w, conv_b, seq):
    m = x2d.shape[0]
    tm = _PROJ_TILE
    widths = (GLA_KEY, GLA_KEY, GLA_VAL, GLA_VAL, GLA_RANK, SSD_INNER, SSD_CONV_DIM, SSD_HEADS,
              2 * D_MODEL)
    halved = (False, False, False, True, False, True, False, False, True)
    col_scale = jnp.concatenate([jnp.full((w,), 0.5 if hv else 1.0, F32) for w, hv in zip(widths, halved)])
    w_bf = (w_in * col_scale[None, :]).astype(BF16)
    o = [0]
    for s in widths:
        o.append(o[-1] + s)
    wq, wk, wv, wg, wf, wz, wxbc, wdt, wgates = (w_bf[:, o[i]:o[i + 1]] for i in range(9))
    w_main = jnp.concatenate([wz, wgates, wv, wg, wq, wk], axis=1)
    w_small = jnp.concatenate(
        [wf, wdt, jnp.zeros((D_MODEL, SMALL_W - GLA_RANK - SSD_HEADS), BF16)], axis=1)
    row = lambda i: (i, 0)
    return pl.pallas_call(
        functools.partial(_inproj_body, tiles_per_row=seq // tm),
        grid=(m // tm,),
        in_specs=[pl.BlockSpec((tm, D_MODEL), row),
                  _resident((1, D_MODEL)), _resident((D_MODEL, P_WIDTH)),
                  _resident((D_MODEL, SMALL_W)), _resident((D_MODEL, SSD_CONV_DIM)),
                  _resident((SSD_CONV, SSD_CONV_DIM)), _resident((1, SSD_CONV_DIM))],
        out_specs=[pl.BlockSpec((tm, P_WIDTH), row),
                   pl.BlockSpec((tm, SMALL_W), row),
                   pl.BlockSpec((tm, SSD_INNER), row),
                   pl.BlockSpec((tm, 2 * SSD_BC), row)],
        out_shape=[jax.ShapeDtypeStruct((m, P_WIDTH), BF16),
                   jax.ShapeDtypeStruct((m, SMALL_W), F32),
                   jax.ShapeDtypeStruct((m, SSD_INNER), F32),
                   jax.ShapeDtypeStruct((m, 2 * SSD_BC), BF16)],
        scratch_shapes=[pltpu.VMEM((_TAIL, SSD_CONV_DIM), F32)],
        compiler_params=pltpu.CompilerParams(dimension_semantics=("arbitrary",),
                                             vmem_limit_bytes=VMEM_LIMIT),
        name="in_proj_conv",
    )(x2d, norm_w.reshape(1, D_MODEL), w_main, w_small, wxbc, 0.5 * conv_w,
      0.5 * conv_b.reshape(1, SSD_CONV_DIM))


_MERGE_PARTS = 2
_PART = SEQ_TILE // _MERGE_PARTS
_PART_CHUNKS = NCHUNK // _MERGE_PARTS
_MERGE_COL_STEP = 256


def _mix_body(q_ref, k_ref, v_ref, g_ref, z_ref, gt_ref, sm_ref, xs_ref, bc_ref, x_ref,
              wf_ref, bf_ref, dtb_ref, alog_ref, e_ref, d_ref, tri_ref,
              wa_ref, wb_ref, wo_ref,
              o_ref,
              gs_ref, ss_ref, vv_ref, *part_refs):
    @pl.when(pl.program_id(1) == 0)
    def _():
        gs_ref[...] = jnp.zeros_like(gs_ref)
        ss_ref[...] = jnp.zeros_like(ss_ref)

    tri = tri_ref[...]
    oa_refs = part_refs[0:_MERGE_PARTS]
    yb_refs = part_refs[_MERGE_PARTS:2 * _MERGE_PARTS]
    mg_refs = part_refs[2 * _MERGE_PARTS:3 * _MERGE_PARTS]

    def chunk_rows(c):
        return slice(c * CHUNK, (c + 1) * CHUNK)

    def part_rows(c):
        c = c % _PART_CHUNKS
        return slice(c * CHUNK, (c + 1) * CHUNK)

    pre = _dot(sm_ref[...].astype(BF16), wf_ref[...]) + bf_ref[...]
    log_a = _log_sigmoid(pre) * (1.0 / GLA_GATE_NORM)
    g_ends = []
    kd = []
    for c in range(NCHUNK):
        rows = chunk_rows(c)
        cum = _cumsum_chunk(tri, log_a[rows])
        end = cum[CHUNK - 1:CHUNK]
        g_ends.append(end)
        kd.append((k_ref[rows, :].astype(F32) * jnp.exp(end - cum)).astype(BF16))
    dec_t = jnp.exp(jnp.concatenate(g_ends, axis=0)).T

    lane = lax.broadcasted_iota(jnp.int32, (1, SMALL_W), 1)
    is_dt = (lane >= SM_DT) & (lane < SM_DT + SSD_HEADS)
    dt = jnp.where(is_dt, _softplus(sm_ref[...] + dtb_ref[...]), 0.0)
    a_neg = jnp.where(is_dt, -jnp.exp(alog_ref[...]), 0.0)
    da = dt * a_neg
    e2 = e_ref[...]
    s_ends = []
    wgt = []
    for c in range(NCHUNK):
        rows = chunk_rows(c)
        cum = _cumsum_chunk(tri, da[rows])
        end = cum[CHUNK - 1:CHUNK]
        s_ends.append(end)
        wgt.append(jnp.exp(end - cum) * dt[rows])
    w_hi, w_lo = _split2(jnp.concatenate(wgt, axis=0))
    w_exp = _dot(jnp.concatenate([w_hi, w_lo], axis=1), e2)
    vv_ref[...] = (w_exp * xs_ref[...]).astype(BF16)
    d_hi, d_mid, d_lo = _split3(jnp.exp(jnp.concatenate(s_ends, axis=0)))
    dec = (_dot(jnp.concatenate([d_hi, d_mid], axis=1), e2)
           + _dot(d_lo, e2[:SMALL_W]))

    def key_cols(h):
        return slice(h * GLA_DK, (h + 1) * GLA_DK)

    def gval_cols(h):
        return slice(h * GLA_DV, (h + 1) * GLA_DV)

    def state_cols(g):
        return slice(g * SSD_STATE, (g + 1) * SSD_STATE)

    def out_cols(g):
        return slice(SSD_BC + g * SSD_STATE, SSD_BC + (g + 1) * SSD_STATE)

    def sval_cols(g):
        return slice(g * SSD_GROUP_W, (g + 1) * SSD_GROUP_W)

    g_upd, g_dec, s_upd = {}, {}, {}

    def issue_updates(c):
        rows = chunk_rows(c)
        for h in range(GLA_HEADS):
            g_upd[c, h] = _dot_t0(kd[c][:, key_cols(h)], v_ref[rows, gval_cols(h)])
            g_dec[c, h] = jnp.broadcast_to(dec_t[key_cols(h), c:c + 1], (GLA_DK, GLA_DV))
        for g in range(SSD_GROUPS):
            s_upd[c, g] = _dot_t0(bc_ref[rows, state_cols(g)], vv_ref[rows, sval_cols(g)])

    g_state = [gs_ref[h] for h in range(GLA_HEADS)]
    s_state = [ss_ref[g] for g in range(SSD_GROUPS)]
    gla_eps = EPS * GLA_DK

    def chunk_step(c):
        rows = chunk_rows(c)
        hh, hrows = c // _PART_CHUNKS, part_rows(c)
        for h in range(GLA_HEADS):
            vc = gval_cols(h)
            g_state[h] = g_state[h] * g_dec.pop((c, h)) + g_upd.pop((c, h))
            o = _dot(q_ref[rows, key_cols(h)], g_state[h].astype(BF16))
            o = o * lax.rsqrt(jnp.mean(o * o, axis=-1, keepdims=True) + gla_eps)
            oa_refs[hh][hrows, vc] = (o * _silu_of_twice(g_ref[rows, vc].astype(F32))).astype(BF16)
        for g in range(SSD_GROUPS):
            vc = sval_cols(g)
            s_state[g] = s_state[g] * dec[c:c + 1, vc] + s_upd.pop((c, g))
            y = _dot(bc_ref[rows, out_cols(g)], s_state[g].astype(BF16))
            y = y + d_ref[:, vc] * xs_ref[rows, vc]
            y = y * _silu_of_twice(z_ref[rows, vc].astype(F32))
            y = y * lax.rsqrt(jnp.mean(y * y, axis=-1, keepdims=True) + EPS)
            yb_refs[hh][hrows, vc] = y.astype(BF16)

    def merge_pieces(hh):
        rows = slice(hh * _PART, (hh + 1) * _PART)
        pieces = []
        for j in range(0, D_MODEL, _MERGE_COL_STEP):
            def gate_block(j=j):
                cs = slice(j, j + _MERGE_COL_STEP)
                cs_b = slice(D_MODEL + j, D_MODEL + j + _MERGE_COL_STEP)
                u_a = _dot(oa_refs[hh][...], wa_ref[:, cs])
                u_b = _dot(yb_refs[hh][...], wb_ref[:, cs])
                merged = ((u_a + jnp.tanh(gt_ref[rows, cs].astype(F32)) * u_a)
                          + (u_b + jnp.tanh(gt_ref[rows, cs_b].astype(F32)) * u_b))
                mg_refs[hh][:, cs] = merged.astype(BF16)
            pieces.append(gate_block)
        for j in range(0, D_MODEL, _MERGE_COL_STEP):
            def out_block(j=j):
                cs = slice(j, j + _MERGE_COL_STEP)
                o_ref[rows, cs] = x_ref[rows, cs] + _dot(mg_refs[hh][...], wo_ref[:, cs])
            pieces.append(out_block)
        return pieces

    for c in range(min(_LOOKAHEAD, NCHUNK)):
        issue_updates(c)
    pending = []
    per_chunk = -(-2 * (D_MODEL // _MERGE_COL_STEP) // _PART_CHUNKS)
    for c in range(NCHUNK):
        if c + _LOOKAHEAD < NCHUNK:
            issue_updates(c + _LOOKAHEAD)
        chunk_step(c)
        for piece in pending[:per_chunk]:
            piece()
        pending = pending[per_chunk:]
        if (c + 1) % _PART_CHUNKS == 0:
            pending = pending + merge_pieces(c // _PART_CHUNKS)
    for piece in pending:
        piece()
    for h in range(GLA_HEADS):
        gs_ref[h] = g_state[h]
    for g in range(SSD_GROUPS):
        ss_ref[g] = s_state[g]


def _mix(x2d, p, small, xs, bc, w_f_up, b_f, gla_norm, dt_bias, a_log, d_skip, ssd_norm,
         gla_w_o, ssd_w_o, w_out, bsz, seq):
    t = SEQ_TILE
    nt = seq // t
    tri = (jnp.arange(CHUNK)[:, None] >= jnp.arange(CHUNK)[None, :]).astype(BF16)
    tri = jnp.concatenate([tri, tri], axis=1)
    wf = jnp.concatenate([w_f_up, jnp.zeros((SMALL_W - GLA_RANK, GLA_KEY), F32)], axis=0).astype(BF16)
    pad_l = jnp.zeros((SM_DT,), F32)
    pad_r = jnp.zeros((SMALL_W - SM_DT - SSD_HEADS,), F32)
    dtb = jnp.concatenate([pad_l, dt_bias, pad_r]).reshape(1, SMALL_W)
    alog = jnp.concatenate([pad_l, a_log, pad_r]).reshape(1, SMALL_W)
    head_of_col = jnp.arange(SSD_INNER, dtype=jnp.int32) // SSD_HEADDIM
    expand = (jnp.arange(SMALL_W, dtype=jnp.int32)[:, None] == head_of_col[None, :] + SM_DT).astype(BF16)
    expand = jnp.concatenate([expand, expand], axis=0)
    d_cols = jnp.repeat(d_skip, SSD_HEADDIM).reshape(1, SSD_INNER)
    wa = (jnp.tile(gla_norm, GLA_HEADS)[:, None] * gla_w_o).astype(BF16)
    wb = (ssd_norm[:, None] * ssd_w_o).astype(BF16)
    wo = (0.5 * w_out).astype(BF16)

    def col(width, off):
        blk = off // width
        return pl.BlockSpec((t, width), lambda b, j: (b * nt + j, blk))

    row = lambda b, j: (b * nt + j, 0)
    return pl.pallas_call(
        _mix_body,
        grid=(bsz, nt),
        in_specs=[col(GLA_KEY, P_Q), col(GLA_KEY, P_K), col(GLA_VAL, P_V), col(GLA_VAL, P_G),
                  col(SSD_INNER, P_Z), col(2 * D_MODEL, P_GATES),
                  pl.BlockSpec((t, SMALL_W), row),
                  pl.BlockSpec((t, SSD_INNER), row),
                  pl.BlockSpec((t, 2 * SSD_BC), row),
                  pl.BlockSpec((t, D_MODEL), row),
                  _resident((SMALL_W, GLA_KEY)), _resident((1, GLA_KEY)),
                  _resident((1, SMALL_W)), _resident((1, SMALL_W)),
                  _resident((2 * SMALL_W, SSD_INNER)), _resident((1, SSD_INNER)),
                  _resident((CHUNK, 2 * CHUNK)),
                  _resident((GLA_VAL, D_MODEL)), _resident((SSD_INNER, D_MODEL)),
                  _resident((D_MODEL, D_MODEL))],
        out_specs=pl.BlockSpec((t, D_MODEL), row),
        out_shape=jax.ShapeDtypeStruct((bsz * seq, D_MODEL), F32),
        scratch_shapes=[pltpu.VMEM((GLA_HEADS, GLA_DK, GLA_DV), F32),
                        pltpu.VMEM((SSD_GROUPS, SSD_STATE, SSD_GROUP_W), F32),
                        pltpu.VMEM((t, SSD_INNER), BF16),
                        *([pltpu.VMEM((_PART, GLA_VAL), BF16)] * _MERGE_PARTS),
                        *([pltpu.VMEM((_PART, SSD_INNER), BF16)] * _MERGE_PARTS),
                        *([pltpu.VMEM((_PART, D_MODEL), BF16)] * _MERGE_PARTS)],
        compiler_params=pltpu.CompilerParams(dimension_semantics=("parallel", "arbitrary"),
                                             vmem_limit_bytes=VMEM_LIMIT),
        name="mix",
    )(p, p, p, p, p, p, small, xs, bc, x2d, wf, b_f.reshape(1, GLA_KEY),
      dtb, alog, expand, d_cols, tri, wa, wb, wo)


def kernel(x, ffn1_norm, ffn1_w_gate, ffn1_w_up, ffn1_w_down, mix_norm, w_in, gla_w_f_up, gla_b_f, gla_norm, gla_w_o, ssd_conv_w, ssd_conv_b, ssd_dt_bias, ssd_a_log, ssd_d, ssd_norm, ssd_w_o, w_out, ffn2_norm, ffn2_w_gate, ffn2_w_up, ffn2_w_down, final_norm):
    bsz, seq, d = x.shape
    depth = ffn1_norm.shape[0]
    assert d == D_MODEL and seq % SEQ_TILE == 0 and depth >= 1
    h = x.reshape(bsz * seq, d)
    for l in range(depth):
        last = l == depth - 1
        h = _ffn(h, ffn1_norm[l], ffn1_w_gate[l], ffn1_w_up[l], ffn1_w_down[l])
        p, small, xs, bc = _in_proj(h, mix_norm[l], w_in[l], ssd_conv_w[l], ssd_conv_b[l], seq)
        h = _mix(h, p, small, xs, bc, gla_w_f_up[l], gla_b_f[l], gla_norm[l], ssd_dt_bias[l],
                 ssd_a_log[l], ssd_d[l], ssd_norm[l], gla_w_o[l], ssd_w_o[l], w_out[l], bsz, seq)
        h = _ffn(h, ffn2_norm[l], ffn2_w_gate[l], ffn2_w_up[l], ffn2_w_down[l],
                 final_w=final_norm if last else None)
    return h.reshape(bsz, seq, d)
```

```python
import functools

import jax
import jax.numpy as jnp
from jax import lax
from jax.experimental import pallas as pl
from jax.experimental.pallas import tpu as pltpu

F32 = jnp.float32
BF16 = jnp.bfloat16

D_MODEL = 1024
CHUNK = 64
EPS = 1e-6

GLA_HEADS = 4
GLA_KEY = 512
GLA_VAL = 1024
GLA_DK = 128
GLA_DV = 256
GLA_RANK = 16
GLA_GATE_NORM = 16.0

SSD_INNER = 2048
SSD_HEADDIM = 64
SSD_HEADS = 32
SSD_STATE = 128
SSD_GROUPS = 8
SSD_GROUP_W = SSD_INNER // SSD_GROUPS
SSD_CONV = 4
SSD_CONV_DIM = 4096
SSD_BC = SSD_GROUPS * SSD_STATE

D_FF = 2816

P_Z = 0
P_GATES = 2048
P_V = 4096
P_G = 5120
P_Q = 6144
P_K = 6656
P_WIDTH = 7168
SMALL_W = 128
SM_F = 0
SM_DT = GLA_RANK

VMEM_LIMIT = 56 * 1024 * 1024

SEQ_TILE = 512
NCHUNK = SEQ_TILE // CHUNK
_LOOKAHEAD = 1


def _resident(shape):
    nd = len(shape)
    return pl.BlockSpec(shape, lambda *_: (0,) * nd, pipeline_mode=pl.Buffered(1))


def _rmsnorm(x, w):
    return x * lax.rsqrt(jnp.mean(x * x, axis=-1, keepdims=True) + EPS) * w


def _softplus(x):
    return jnp.maximum(x, 0.0) + jnp.log1p(jnp.exp(-jnp.abs(x)))


def _log_sigmoid(x):
    return jnp.minimum(x, 0.0) - jnp.log(1.0 + jnp.exp(-jnp.abs(x)))


def _silu_of_twice(hx):
    return hx + hx * jnp.tanh(hx)


def _dot(a, b):
    return jnp.dot(a, b, preferred_element_type=F32)


def _dot_t0(a, b):
    return lax.dot_general(a, b, (((0,), (0,)), ((), ())), preferred_element_type=F32)


def _split2(x):
    hi = x.astype(BF16)
    lo = (x - hi.astype(F32)).astype(BF16)
    return hi, lo


def _split3(x):
    hi = x.astype(BF16)
    r = x - hi.astype(F32)
    mid = r.astype(BF16)
    lo = (r - mid.astype(F32)).astype(BF16)
    return hi, mid, lo


def _cumsum_chunk(tri2, x):
    hi, lo = _split2(x)
    return _dot(tri2, jnp.concatenate([hi, lo], axis=0))


def _ffn_body(x_ref, nw_ref, wg_ref, wu_ref, wd_ref, *rest, final_norm):
    if final_norm:
        fw_ref, o_ref = rest
    else:
        (o_ref,) = rest
    x = x_ref[...]
    h = _rmsnorm(x, nw_ref[...]).astype(BF16)
    g = _dot(h, wg_ref[...])
    u = _dot(h, wu_ref[...])
    act = (_silu_of_twice(g) * u).astype(BF16)
    out = x + 0.5 * _dot(act, wd_ref[...])
    if final_norm:
        out = _rmsnorm(out, fw_ref[...])
    o_ref[...] = out


def _ffn(x2d, norm_w, w_gate, w_up, w_down, final_w=None, tm=512):
    m = x2d.shape[0]
    ins = [x2d, norm_w.reshape(1, D_MODEL), (0.5 * w_gate).astype(BF16), w_up.astype(BF16),
           w_down.astype(BF16)]
    specs = [pl.BlockSpec((tm, D_MODEL), lambda i: (i, 0)),
             _resident((1, D_MODEL)), _resident((D_MODEL, D_FF)), _resident((D_MODEL, D_FF)),
             _resident((D_FF, D_MODEL))]
    if final_w is not None:
        ins.append(final_w.reshape(1, D_MODEL))
        specs.append(_resident((1, D_MODEL)))
    return pl.pallas_call(
        functools.partial(_ffn_body, final_norm=final_w is not None),
        grid=(m // tm,),
        in_specs=specs,
        out_specs=pl.BlockSpec((tm, D_MODEL), lambda i: (i, 0)),
        out_shape=jax.ShapeDtypeStruct((m, D_MODEL), F32),
        compiler_params=pltpu.CompilerParams(dimension_semantics=("parallel",),
                                             vmem_limit_bytes=VMEM_LIMIT),
        name="ffn_final" if final_w is not None else "ffn",
    )(*ins)


_PROJ_COL_STEP = 512
_PROJ_TILE = 256
_CONV_COL_STEP = 256
_TAIL = 8


def _inproj_body(x_ref, nw_ref, w_ref, ws_ref, wx_ref, cw_ref, cb_ref,
                 p_ref, s_ref, xs_ref, bc_ref, tail_ref, *, tiles_per_row):
    tm = _PROJ_TILE

    @pl.when(lax.rem(pl.program_id(0), tiles_per_row) == 0)
    def _():
        tail_ref[...] = jnp.zeros_like(tail_ref)

    h = _rmsnorm(x_ref[...], nw_ref[...]).astype(BF16)
    row = lax.broadcasted_iota(jnp.int32, (_TAIL, _CONV_COL_STEP), 0)

    def slab_block(a):
        p_ref[:, a:a + _PROJ_COL_STEP] = _dot(h, w_ref[:, a:a + _PROJ_COL_STEP]).astype(BF16)

    def xbc_block(a):
        cs = slice(a, a + _CONV_COL_STEP)
        r = _dot(h, wx_ref[:, cs])
        tail = tail_ref[:, cs]
        w0, w1, w2, w3 = (cw_ref[i:i + 1, cs] for i in range(SSD_CONV))

        def delay(cur, prev, k):
            rolled = pltpu.roll(cur, k, axis=0)
            head = jnp.where(row < k, pltpu.roll(prev, k, axis=0), rolled[0:_TAIL])
            return jnp.concatenate([head, rolled[_TAIL:]], axis=0)

        r1 = delay(r, tail, 1)
        t1 = pltpu.roll(tail, 1, axis=0)
        b = r * w1 + r1 * w0
        b_tail = tail * w1 + t1 * w0
        acc = (r * w3 + r1 * w2) + delay(b, b_tail, 2)
        tail_ref[:, cs] = r[tm - _TAIL:tm]
        act = _silu_of_twice(acc + cb_ref[:, cs])
        if a < SSD_INNER:
            xs_ref[:, cs] = act
        else:
            bc_ref[:, a - SSD_INNER:a - SSD_INNER + _CONV_COL_STEP] = act.astype(BF16)

    plain = [functools.partial(slab_block, a) for a in range(0, P_WIDTH, _PROJ_COL_STEP)]
    fused = [functools.partial(xbc_block, a) for a in range(0, SSD_CONV_DIM, _CONV_COL_STEP)]
    done = 0
    for i, task in enumerate(fused):
        task()
        want = ((i + 1) * len(plain)) // len(fused)
        for j in range(done, want):
            plain[j]()
        done = want
    s_ref[...] = _dot(h, ws_ref[...])


def _in_proj(x2d, norm_w, w_in, conv_w, conv_b, seq):
    m = x2d.shape[0]
    tm = _PROJ_TILE
    widths = (GLA_KEY, GLA_KEY, GLA_VAL, GLA_VAL, GLA_RANK, SSD_INNER, SSD_CONV_DIM, SSD_HEADS,
              2 * D_MODEL)
    halved = (False, False, False, True, False, True, False, False, True)
    col_scale = jnp.concatenate([jnp.full((w,), 0.5 if hv else 1.0, F32) for w, hv in zip(widths, halved)])
    w_bf = (w_in * col_scale[None, :]).astype(BF16)
    o = [0]
    for s in widths:
        o.append(o[-1] + s)
    wq, wk, wv, wg, wf, wz, wxbc, wdt, wgates = (w_bf[:, o[i]:o[i + 1]] for i in range(9))
    w_main = jnp.concatenate([wz, wgates, wv, wg, wq, wk], axis=1)
    w_small = jnp.concatenate(
        [wf, wdt, jnp.zeros((D_MODEL, SMALL_W - GLA_RANK - SSD_HEADS), BF16)], axis=1)
    row = lambda i: (i, 0)
    return pl.pallas_call(
        functools.partial(_inproj_body, tiles_per_row=seq // tm),
        grid=(m // tm,),
        in_specs=[pl.BlockSpec((tm, D_MODEL), row),
                  _resident((1, D_MODEL)), _resident((D_MODEL, P_WIDTH)),
                  _resident((D_MODEL, SMALL_W)), _resident((D_MODEL, SSD_CONV_DIM)),
                  _resident((SSD_CONV, SSD_CONV_DIM)), _resident((1, SSD_CONV_DIM))],
        out_specs=[pl.BlockSpec((tm, P_WIDTH), row),
                   pl.BlockSpec((tm, SMALL_W), row),
                   pl.BlockSpec((tm, SSD_INNER), row),
                   pl.BlockSpec((tm, 2 * SSD_BC), row)],
        out_shape=[jax.ShapeDtypeStruct((m, P_WIDTH), BF16),
                   jax.ShapeDtypeStruct((m, SMALL_W), F32),
                   jax.ShapeDtypeStruct((m, SSD_INNER), F32),
                   jax.ShapeDtypeStruct((m, 2 * SSD_BC), BF16)],
        scratch_shapes=[pltpu.VMEM((_TAIL, SSD_CONV_DIM), F32)],
        compiler_params=pltpu.CompilerParams(dimension_semantics=("arbitrary",),
                                             vmem_limit_bytes=VMEM_LIMIT),
        name="in_proj_conv",
    )(x2d, norm_w.reshape(1, D_MODEL), w_main, w_small, wxbc, 0.5 * conv_w,
      0.5 * conv_b.reshape(1, SSD_CONV_DIM))


_MERGE_PARTS = 2
_PART = SEQ_TILE // _MERGE_PARTS
_PART_CHUNKS = NCHUNK // _MERGE_PARTS
_MERGE_COL_STEP = 512


def _mix_body(q_ref, k_ref, v_ref, g_ref, z_ref, gt_ref, sm_ref, xs_ref, bc_ref, x_ref,
              wf_ref, bf_ref, dtb_ref, alog_ref, e_ref, d_ref, tri_ref,
              wa_ref, wb_ref, wo_ref,
              o_ref,
              gs_ref, ss_ref, vv_ref, *part_refs):
    @pl.when(pl.program_id(1) == 0)
    def _():
        gs_ref[...] = jnp.zeros_like(gs_ref)
        ss_ref[...] = jnp.zeros_like(ss_ref)

    tri = tri_ref[...]
    oa_refs = part_refs[0:_MERGE_PARTS]
    yb_refs = part_refs[_MERGE_PARTS:2 * _MERGE_PARTS]
    mg_refs = part_refs[2 * _MERGE_PARTS:3 * _MERGE_PARTS]

    def chunk_rows(c):
        return slice(c * CHUNK, (c + 1) * CHUNK)

    def part_rows(c):
        c = c % _PART_CHUNKS
        return slice(c * CHUNK, (c + 1) * CHUNK)

    pre = _dot(sm_ref[...].astype(BF16), wf_ref[...]) + bf_ref[...]
    log_a = _log_sigmoid(pre) * (1.0 / GLA_GATE_NORM)
    g_ends = []
    kd = []
    for c in range(NCHUNK):
        rows = chunk_rows(c)
        cum = _cumsum_chunk(tri, log_a[rows])
        end = cum[CHUNK - 1:CHUNK]
        g_ends.append(end)
        kd.append((k_ref[rows, :].astype(F32) * jnp.exp(end - cum)).astype(BF16))
    dec_t = jnp.exp(jnp.concatenate(g_ends, axis=0)).T

    lane = lax.broadcasted_iota(jnp.int32, (1, SMALL_W), 1)
    is_dt = (lane >= SM_DT) & (lane < SM_DT + SSD_HEADS)
    dt = jnp.where(is_dt, _softplus(sm_ref[...] + dtb_ref[...]), 0.0)
    a_neg = jnp.where(is_dt, -jnp.exp(alog_ref[...]), 0.0)
    da = dt * a_neg
    e2 = e_ref[...]
    s_ends = []
    wgt = []
    for c in range(NCHUNK):
        rows = chunk_rows(c)
        cum = _cumsum_chunk(tri, da[rows])
        end = cum[CHUNK - 1:CHUNK]
        s_ends.append(end)
        wgt.append(jnp.exp(end - cum) * dt[rows])
    w_hi, w_lo = _split2(jnp.concatenate(wgt, axis=0))
    w_exp = _dot(jnp.concatenate([w_hi, w_lo], axis=1), e2)
    vv_ref[...] = (w_exp * xs_ref[...]).astype(BF16)
    d_hi, d_mid, d_lo = _split3(jnp.exp(jnp.concatenate(s_ends, axis=0)))
    dec = (_dot(jnp.concatenate([d_hi, d_mid], axis=1), e2)
           + _dot(d_lo, e2[:SMALL_W]))

    def key_cols(h):
        return slice(h * GLA_DK, (h + 1) * GLA_DK)

    def gval_cols(h):
        return slice(h * GLA_DV, (h + 1) * GLA_DV)

    def state_cols(g):
        return slice(g * SSD_STATE, (g + 1) * SSD_STATE)

    def out_cols(g):
        return slice(SSD_BC + g * SSD_STATE, SSD_BC + (g + 1) * SSD_STATE)

    def sval_cols(g):
        return slice(g * SSD_GROUP_W, (g + 1) * SSD_GROUP_W)

    g_upd, g_dec, s_upd = {}, {}, {}

    units = [("gla", h) for h in range(GLA_HEADS)] + [("ssd", g) for g in range(SSD_GROUPS)]

    def issue_updates(unit, chunks):
        kind, i = unit
        for c in chunks:
            rows = chunk_rows(c)
            if kind == "gla":
                g_upd[c, i] = _dot_t0(kd[c][:, key_cols(i)], v_ref[rows, gval_cols(i)])
                g_dec[c, i] = jnp.broadcast_to(dec_t[key_cols(i), c:c + 1], (GLA_DK, GLA_DV))
            else:
                s_upd[c, i] = _dot_t0(bc_ref[rows, state_cols(i)], vv_ref[rows, sval_cols(i)])

    g_state = [gs_ref[h] for h in range(GLA_HEADS)]
    s_state = [ss_ref[g] for g in range(SSD_GROUPS)]
    gla_eps = EPS * GLA_DK

    def gla_step(c, h):
        rows = chunk_rows(c)
        hh, hrows = c // _PART_CHUNKS, part_rows(c)
        vc = gval_cols(h)
        g_state[h] = g_state[h] * g_dec.pop((c, h)) + g_upd.pop((c, h))
        o = _dot(q_ref[rows, key_cols(h)], g_state[h].astype(BF16))
        o = o * lax.rsqrt(jnp.mean(o * o, axis=-1, keepdims=True) + gla_eps)
        oa_refs[hh][hrows, vc] = (o * _silu_of_twice(g_ref[rows, vc].astype(F32))).astype(BF16)

    def ssd_step(c, g):
        rows = chunk_rows(c)
        hh, hrows = c // _PART_CHUNKS, part_rows(c)
        vc = sval_cols(g)
        s_state[g] = s_state[g] * dec[c:c + 1, vc] + s_upd.pop((c, g))
        y = _dot(bc_ref[rows, out_cols(g)], s_state[g].astype(BF16))
        y = y + d_ref[:, vc] * xs_ref[rows, vc]
        y = y * _silu_of_twice(z_ref[rows, vc].astype(F32))
        y = y * lax.rsqrt(jnp.mean(y * y, axis=-1, keepdims=True) + EPS)
        yb_refs[hh][hrows, vc] = y.astype(BF16)

    def merge_pieces(hh):
        rows = slice(hh * _PART, (hh + 1) * _PART)
        pieces = []
        for j in range(0, D_MODEL, _MERGE_COL_STEP):
            def gate_block(j=j):
                cs = slice(j, j + _MERGE_COL_STEP)
                cs_b = slice(D_MODEL + j, D_MODEL + j + _MERGE_COL_STEP)
                u_a = _dot(oa_refs[hh][...], wa_ref[:, cs])
                u_b = _dot(yb_refs[hh][...], wb_ref[:, cs])
                merged = ((u_a + jnp.tanh(gt_ref[rows, cs].astype(F32)) * u_a)
                          + (u_b + jnp.tanh(gt_ref[rows, cs_b].astype(F32)) * u_b))
                mg_refs[hh][:, cs] = merged.astype(BF16)
            pieces.append(gate_block)
        for j in range(0, D_MODEL, _MERGE_COL_STEP):
            def out_block(j=j):
                cs = slice(j, j + _MERGE_COL_STEP)
                o_ref[rows, cs] = x_ref[rows, cs] + _dot(mg_refs[hh][...], wo_ref[:, cs])
            pieces.append(out_block)
        return pieces

    pending = []
    for part in range(_MERGE_PARTS):
        chunks = range(part * _PART_CHUNKS, (part + 1) * _PART_CHUNKS)
        for unit in units[:_LOOKAHEAD]:
            issue_updates(unit, chunks)
        for ui, unit in enumerate(units):
            if ui + _LOOKAHEAD < len(units):
                issue_updates(units[ui + _LOOKAHEAD], chunks)
            for c in chunks:
                (gla_step if unit[0] == "gla" else ssd_step)(c, unit[1])
            for piece in pending[:1]:
                piece()
            pending = pending[1:]
        pending = pending + merge_pieces(part)
    for piece in pending:
        piece()
    for h in range(GLA_HEADS):
        gs_ref[h] = g_state[h]
    for g in range(SSD_GROUPS):
        ss_ref[g] = s_state[g]


def _mix(x2d, p, small, xs, bc, w_f_up, b_f, gla_norm, dt_bias, a_log, d_skip, ssd_norm,
         gla_w_o, ssd_w_o, w_out, bsz, seq):
    t = SEQ_TILE
    nt = seq // t
    tri = (jnp.arange(CHUNK)[:, None] >= jnp.arange(CHUNK)[None, :]).astype(BF16)
    tri = jnp.concatenate([tri, tri], axis=1)
    wf = jnp.concatenate([w_f_up, jnp.zeros((SMALL_W - GLA_RANK, GLA_KEY), F32)], axis=0).astype(BF16)
    pad_l = jnp.zeros((SM_DT,), F32)
    pad_r = jnp.zeros((SMALL_W - SM_DT - SSD_HEADS,), F32)
    dtb = jnp.concatenate([pad_l, dt_bias, pad_r]).reshape(1, SMALL_W)
    alog = jnp.concatenate([pad_l, a_log, pad_r]).reshape(1, SMALL_W)
    head_of_col = jnp.arange(SSD_INNER, dtype=jnp.int32) // SSD_HEADDIM
    expand = (jnp.arange(SMALL_W, dtype=jnp.int32)[:, None] == head_of_col[None, :] + SM_DT).astype(BF16)
    expand = jnp.concatenate([expand, expand], axis=0)
    d_cols = jnp.repeat(d_skip, SSD_HEADDIM).reshape(1, SSD_INNER)
    wa = (jnp.tile(gla_norm, GLA_HEADS)[:, None] * gla_w_o).astype(BF16)
    wb = (ssd_norm[:, None] * ssd_w_o).astype(BF16)
    wo = (0.5 * w_out).astype(BF16)

    def col(width, off):
        blk = off // width
        return pl.BlockSpec((t, width), lambda b, j: (b * nt + j, blk))

    row = lambda b, j: (b * nt + j, 0)
    return pl.pallas_call(
        _mix_body,
        grid=(bsz, nt),
        in_specs=[col(GLA_KEY, P_Q), col(GLA_KEY, P_K), col(GLA_VAL, P_V), col(GLA_VAL, P_G),
                  col(SSD_INNER, P_Z), col(2 * D_MODEL, P_GATES),
                  pl.BlockSpec((t, SMALL_W), row),
                  pl.BlockSpec((t, SSD_INNER), row),
                  pl.BlockSpec((t, 2 * SSD_BC), row),
                  pl.BlockSpec((t, D_MODEL), row),
                  _resident((SMALL_W, GLA_KEY)), _resident((1, GLA_KEY)),
                  _resident((1, SMALL_W)), _resident((1, SMALL_W)),
                  _resident((2 * SMALL_W, SSD_INNER)), _resident((1, SSD_INNER)),
                  _resident((CHUNK, 2 * CHUNK)),
                  _resident((GLA_VAL, D_MODEL)), _resident((SSD_INNER, D_MODEL)),
                  _resident((D_MODEL, D_MODEL))],
        out_specs=pl.BlockSpec((t, D_MODEL), row),
        out_shape=jax.ShapeDtypeStruct((bsz * seq, D_MODEL), F32),
        scratch_shapes=[pltpu.VMEM((GLA_HEADS, GLA_DK, GLA_DV), F32),
                        pltpu.VMEM((SSD_GROUPS, SSD_STATE, SSD_GROUP_W), F32),
                        pltpu.VMEM((t, SSD_INNER), BF16),
                        *([pltpu.VMEM((_PART, GLA_VAL), BF16)] * _MERGE_PARTS),
                        *([pltpu.VMEM((_PART, SSD_INNER), BF16)] * _MERGE_PARTS),
                        *([pltpu.VMEM((_PART, D_MODEL), BF16)] * _MERGE_PARTS)],
        compiler_params=pltpu.CompilerParams(dimension_semantics=("parallel", "arbitrary"),
                                             vmem_limit_bytes=VMEM_LIMIT),
        name="mix",
    )(p, p, p, p, p, p, small, xs, bc, x2d, wf, b_f.reshape(1, GLA_KEY),
      dtb, alog, expand, d_cols, tri, wa, wb, wo)


def kernel(x, ffn1_norm, ffn1_w_gate, ffn1_w_up, ffn1_w_down, mix_norm, w_in, gla_w_f_up, gla_b_f, gla_norm, gla_w_o, ssd_conv_w, ssd_conv_b, ssd_dt_bias, ssd_a_log, ssd_d, ssd_norm, ssd_w_o, w_out, ffn2_norm, ffn2_w_gate, ffn2_w_up, ffn2_w_down, final_norm):
    bsz, seq, d = x.shape
    depth = ffn1_norm.shape[0]
    assert d == D_MODEL and seq % SEQ_TILE == 0 and depth >= 1
    h = x.reshape(bsz * seq, d)
    for l in range(depth):
        last = l == depth - 1
        h = _ffn(h, ffn1_norm[l], ffn1_w_gate[l], ffn1_w_up[l], ffn1_w_down[l])
        p, small, xs, bc = _in_proj(h, mix_norm[l], w_in[l], ssd_conv_w[l], ssd_conv_b[l], seq)
        h = _mix(h, p, small, xs, bc, gla_w_f_up[l], gla_b_f[l], gla_norm[l], ssd_dt_bias[l],
                 ssd_a_log[l], ssd_d[l], ssd_norm[l], gla_w_o[l], ssd_w_o[l], w_out[l], bsz, seq)
        h = _ffn(h, ffn2_norm[l], ffn2_w_gate[l], ffn2_w_up[l], ffn2_w_down[l],
                 final_w=final_norm if last else None)
    return h.reshape(bsz, seq, d)
```

```python
import functools

import jax
import jax.numpy as jnp
from jax import lax
from jax.experimental import pallas as pl
from jax.experimental.pallas import tpu as pltpu

F32 = jnp.float32
BF16 = jnp.bfloat16

D_MODEL = 1024
CHUNK = 64
EPS = 1e-6

GLA_HEADS = 4
GLA_KEY = 512
GLA_VAL = 1024
GLA_DK = 128
GLA_DV = 256
GLA_RANK = 16
GLA_GATE_NORM = 16.0

SSD_INNER = 2048
SSD_HEADDIM = 64
SSD_HEADS = 32
SSD_STATE = 128
SSD_GROUPS = 8
SSD_GROUP_W = SSD_INNER // SSD_GROUPS
SSD_CONV = 4
SSD_CONV_DIM = 4096
SSD_BC = SSD_GROUPS * SSD_STATE

D_FF = 2816

P_Z = 0
P_GATES = 2048
P_V = 4096
P_G = 5120
P_Q = 6144
P_K = 6656
P_WIDTH = 7168
SMALL_W = 128
SM_F = 0
SM_DT = GLA_RANK

VMEM_LIMIT = 56 * 1024 * 1024

SEQ_TILE = 512
NCHUNK = SEQ_TILE // CHUNK
_LOOKAHEAD = 1


def _resident(shape):
    nd = len(shape)
    return pl.BlockSpec(shape, lambda *_: (0,) * nd, pipeline_mode=pl.Buffered(1))


def _rmsnorm(x, w):
    return x * lax.rsqrt(jnp.mean(x * x, axis=-1, keepdims=True) + EPS) * w


def _softplus(x):
    return jnp.maximum(x, 0.0) + jnp.log1p(jnp.exp(-jnp.abs(x)))


def _log_sigmoid(x):
    return jnp.minimum(x, 0.0) - jnp.log(1.0 + jnp.exp(-jnp.abs(x)))


def _silu_of_twice(hx):
    return hx + hx * jnp.tanh(hx)


def _dot(a, b):
    return jnp.dot(a, b, preferred_element_type=F32)


def _dot_t0(a, b):
    return lax.dot_general(a, b, (((0,), (0,)), ((), ())), preferred_element_type=F32)


def _split2(x):
    hi = x.astype(BF16)
    lo = (x - hi.astype(F32)).astype(BF16)
    return hi, lo


def _split3(x):
    hi = x.astype(BF16)
    r = x - hi.astype(F32)
    mid = r.astype(BF16)
    lo = (r - mid.astype(F32)).astype(BF16)
    return hi, mid, lo


def _cumsum_chunk(tri2, x):
    hi, lo = _split2(x)
    return _dot(tri2, jnp.concatenate([hi, lo], axis=0))


def _ffn_body(x_ref, nw_ref, wg_ref, wu_ref, wd_ref, *rest, final_norm):
    if final_norm:
        fw_ref, o_ref = rest
    else:
        (o_ref,) = rest
    x = x_ref[...]
    h = _rmsnorm(x, nw_ref[...]).astype(BF16)
    g = _dot(h, wg_ref[...])
    u = _dot(h, wu_ref[...])
    act = (_silu_of_twice(g) * u).astype(BF16)
    out = x + 0.5 * _dot(act, wd_ref[...])
    if final_norm:
        out = _rmsnorm(out, fw_ref[...])
    o_ref[...] = out


def _ffn(x2d, norm_w, w_gate, w_up, w_down, final_w=None, tm=512):
    m = x2d.shape[0]
    ins = [x2d, norm_w.reshape(1, D_MODEL), (0.5 * w_gate).astype(BF16), w_up.astype(BF16),
           w_down.astype(BF16)]
    specs = [pl.BlockSpec((tm, D_MODEL), lambda i: (i, 0)),
             _resident((1, D_MODEL)), _resident((D_MODEL, D_FF)), _resident((D_MODEL, D_FF)),
             _resident((D_FF, D_MODEL))]
    if final_w is not None:
        ins.append(final_w.reshape(1, D_MODEL))
        specs.append(_resident((1, D_MODEL)))
    return pl.pallas_call(
        functools.partial(_ffn_body, final_norm=final_w is not None),
        grid=(m // tm,),
        in_specs=specs,
        out_specs=pl.BlockSpec((tm, D_MODEL), lambda i: (i, 0)),
        out_shape=jax.ShapeDtypeStruct((m, D_MODEL), F32),
        compiler_params=pltpu.CompilerParams(dimension_semantics=("parallel",),
                                             vmem_limit_bytes=VMEM_LIMIT),
        name="ffn_final" if final_w is not None else "ffn",
    )(*ins)


_PROJ_COL_STEP = 512
_PROJ_TILE = 256
_CONV_COL_STEP = 256
_TAIL = 8


def _inproj_body(x_ref, nw_ref, w_ref, ws_ref, wx_ref, cw_ref, cb_ref,
                 p_ref, s_ref, xs_ref, bc_ref, tail_ref, *, tiles_per_row):
    tm = _PROJ_TILE

    @pl.when(lax.rem(pl.program_id(0), tiles_per_row) == 0)
    def _():
        tail_ref[...] = jnp.zeros_like(tail_ref)

    h = _rmsnorm(x_ref[...], nw_ref[...]).astype(BF16)
    row = lax.broadcasted_iota(jnp.int32, (_TAIL, _CONV_COL_STEP), 0)

    def slab_block(a):
        p_ref[:, a:a + _PROJ_COL_STEP] = _dot(h, w_ref[:, a:a + _PROJ_COL_STEP]).astype(BF16)

    def xbc_dot(a):
        return _dot(h, wx_ref[:, a:a + _CONV_COL_STEP])

    def xbc_epilogue(a, r):
        cs = slice(a, a + _CONV_COL_STEP)
        tail = tail_ref[:, cs]
        w0, w1, w2, w3 = (cw_ref[i:i + 1, cs] for i in range(SSD_CONV))

        def delay(cur, prev, k):
            rolled = pltpu.roll(cur, k, axis=0)
            head = jnp.where(row < k, pltpu.roll(prev, k, axis=0), rolled[0:_TAIL])
            return jnp.concatenate([head, rolled[_TAIL:]], axis=0)

        r1 = delay(r, tail, 1)
        t1 = pltpu.roll(tail, 1, axis=0)
        b = r * w1 + r1 * w0
        b_tail = tail * w1 + t1 * w0
        acc = (r * w3 + r1 * w2) + delay(b, b_tail, 2)
        tail_ref[:, cs] = r[tm - _TAIL:tm]
        act = _silu_of_twice(acc + cb_ref[:, cs])
        if a < SSD_INNER:
            xs_ref[:, cs] = act
        else:
            bc_ref[:, a - SSD_INNER:a - SSD_INNER + _CONV_COL_STEP] = act.astype(BF16)

    plain = [functools.partial(slab_block, a) for a in range(0, P_WIDTH, _PROJ_COL_STEP)]
    starts = list(range(0, SSD_CONV_DIM, _CONV_COL_STEP))
    done = 0
    r_next = xbc_dot(starts[0])
    for i, a in enumerate(starts):
        r_cur = r_next
        if i + 1 < len(starts):
            r_next = xbc_dot(starts[i + 1])
        want = ((i + 1) * len(plain)) // len(starts)
        for j in range(done, want):
            plain[j]()
        done = want
        xbc_epilogue(a, r_cur)
    s_ref[...] = _dot(h, ws_ref[...])


def _in_proj(x2d, norm_w, w_in, conv_w, conv_b, seq):
    m = x2d.shape[0]
    tm = _PROJ_TILE
    widths = (GLA_KEY, GLA_KEY, GLA_VAL, GLA_VAL, GLA_RANK, SSD_INNER, SSD_CONV_DIM, SSD_HEADS,
              2 * D_MODEL)
    halved = (False, False, False, True, False, True, False, False, True)
    col_scale = jnp.concatenate([jnp.full((w,), 0.5 if hv else 1.0, F32) for w, hv in zip(widths, halved)])
    w_bf = (w_in * col_scale[None, :]).astype(BF16)
    o = [0]
    for s in widths:
        o.append(o[-1] + s)
    wq, wk, wv, wg, wf, wz, wxbc, wdt, wgates = (w_bf[:, o[i]:o[i + 1]] for i in range(9))
    w_main = jnp.concatenate([wz, wgates, wv, wg, wq, wk], axis=1)
    w_small = jnp.concatenate(
        [wf, wdt, jnp.zeros((D_MODEL, SMALL_W - GLA_RANK - SSD_HEADS), BF16)], axis=1)
    row = lambda i: (i, 0)
    return pl.pallas_call(
        functools.partial(_inproj_body, tiles_per_row=seq // tm),
        grid=(m // tm,),
        in_specs=[pl.BlockSpec((tm, D_MODEL), row),
                  _resident((1, D_MODEL)), _resident((D_MODEL, P_WIDTH)),
                  _resident((D_MODEL, SMALL_W)), _resident((D_MODEL, SSD_CONV_DIM)),
                  _resident((SSD_CONV, SSD_CONV_DIM)), _resident((1, SSD_CONV_DIM))],
        out_specs=[pl.BlockSpec((tm, P_WIDTH), row),
                   pl.BlockSpec((tm, SMALL_W), row),
                   pl.BlockSpec((tm, SSD_INNER), row),
                   pl.BlockSpec((tm, 2 * SSD_BC), row)],
        out_shape=[jax.ShapeDtypeStruct((m, P_WIDTH), BF16),
                   jax.ShapeDtypeStruct((m, SMALL_W), F32),
                   jax.ShapeDtypeStruct((m, SSD_INNER), F32),
                   jax.ShapeDtypeStruct((m, 2 * SSD_BC), BF16)],
        scratch_shapes=[pltpu.VMEM((_TAIL, SSD_CONV_DIM), F32)],
        compiler_params=pltpu.CompilerParams(dimension_semantics=("arbitrary",),
                                             vmem_limit_bytes=VMEM_LIMIT),
        name="in_proj_conv",
    )(x2d, norm_w.reshape(1, D_MODEL), w_main, w_small, wxbc, 0.5 * conv_w,
      0.5 * conv_b.reshape(1, SSD_CONV_DIM))


_MERGE_PARTS = 2
_PART = SEQ_TILE // _MERGE_PARTS
_PART_CHUNKS = NCHUNK // _MERGE_PARTS
_MERGE_COL_STEP = 512


def _mix_body(q_ref, k_ref, v_ref, g_ref, z_ref, gt_ref, sm_ref, xs_ref, bc_ref, x_ref,
              wf_ref, bf_ref, dtb_ref, alog_ref, e_ref, d_ref, tri_ref,
              wa_ref, wb_ref, wo_ref,
              o_ref,
              gs_ref, ss_ref, vv_ref, *part_refs):
    @pl.when(pl.program_id(1) == 0)
    def _():
        gs_ref[...] = jnp.zeros_like(gs_ref)
        ss_ref[...] = jnp.zeros_like(ss_ref)

    tri = tri_ref[...]
    oa_refs = part_refs[0:_MERGE_PARTS]
    yb_refs = part_refs[_MERGE_PARTS:2 * _MERGE_PARTS]
    mg_refs = part_refs[2 * _MERGE_PARTS:3 * _MERGE_PARTS]

    def chunk_rows(c):
        return slice(c * CHUNK, (c + 1) * CHUNK)

    def part_rows(c):
        c = c % _PART_CHUNKS
        return slice(c * CHUNK, (c + 1) * CHUNK)

    pre = _dot(sm_ref[...].astype(BF16), wf_ref[...]) + bf_ref[...]
    log_a = _log_sigmoid(pre) * (1.0 / GLA_GATE_NORM)
    g_ends = []
    kd = []
    for c in range(NCHUNK):
        rows = chunk_rows(c)
        cum = _cumsum_chunk(tri, log_a[rows])
        end = cum[CHUNK - 1:CHUNK]
        g_ends.append(end)
        kd.append((k_ref[rows, :].astype(F32) * jnp.exp(end - cum)).astype(BF16))
    dec_t = jnp.exp(jnp.concatenate(g_ends, axis=0)).T

    lane = lax.broadcasted_iota(jnp.int32, (1, SMALL_W), 1)
    is_dt = (lane >= SM_DT) & (lane < SM_DT + SSD_HEADS)
    dt = jnp.where(is_dt, _softplus(sm_ref[...] + dtb_ref[...]), 0.0)
    a_neg = jnp.where(is_dt, -jnp.exp(alog_ref[...]), 0.0)
    da = dt * a_neg
    e2 = e_ref[...]
    s_ends = []
    wgt = []
    for c in range(NCHUNK):
        rows = chunk_rows(c)
        cum = _cumsum_chunk(tri, da[rows])
        end = cum[CHUNK - 1:CHUNK]
        s_ends.append(end)
        wgt.append(jnp.exp(end - cum) * dt[rows])
    w_hi, w_lo = _split2(jnp.concatenate(wgt, axis=0))
    w_exp = _dot(jnp.concatenate([w_hi, w_lo], axis=1), e2)
    vv_ref[...] = (w_exp * xs_ref[...]).astype(BF16)
    d_hi, d_mid, d_lo = _split3(jnp.exp(jnp.concatenate(s_ends, axis=0)))
    dec = (_dot(jnp.concatenate([d_hi, d_mid], axis=1), e2)
           + _dot(d_lo, e2[:SMALL_W]))

    def key_cols(h):
        return slice(h * GLA_DK, (h + 1) * GLA_DK)

    def gval_cols(h):
        return slice(h * GLA_DV, (h + 1) * GLA_DV)

    def state_cols(g):
        return slice(g * SSD_STATE, (g + 1) * SSD_STATE)

    def out_cols(g):
        return slice(SSD_BC + g * SSD_STATE, SSD_BC + (g + 1) * SSD_STATE)

    def sval_cols(g):
        return slice(g * SSD_GROUP_W, (g + 1) * SSD_GROUP_W)

    g_upd, g_dec, s_upd = {}, {}, {}

    units = [("gla", h) for h in range(GLA_HEADS)] + [("ssd", g) for g in range(SSD_GROUPS)]

    def issue_updates(unit, chunks):
        kind, i = unit
        for c in chunks:
            rows = chunk_rows(c)
            if kind == "gla":
                g_upd[c, i] = _dot_t0(kd[c][:, key_cols(i)], v_ref[rows, gval_cols(i)])
                g_dec[c, i] = jnp.broadcast_to(dec_t[key_cols(i), c:c + 1], (GLA_DK, GLA_DV))
            else:
                s_upd[c, i] = _dot_t0(bc_ref[rows, state_cols(i)], vv_ref[rows, sval_cols(i)])

    g_state = [gs_ref[h] for h in range(GLA_HEADS)]
    s_state = [ss_ref[g] for g in range(SSD_GROUPS)]
    gla_eps = EPS * GLA_DK

    def gla_step(c, h):
        rows = chunk_rows(c)
        hh, hrows = c // _PART_CHUNKS, part_rows(c)
        vc = gval_cols(h)
        g_state[h] = g_state[h] * g_dec.pop((c, h)) + g_upd.pop((c, h))
        o = _dot(q_ref[rows, key_cols(h)], g_state[h].astype(BF16))
        o = o * lax.rsqrt(jnp.mean(o * o, axis=-1, keepdims=True) + gla_eps)
        oa_refs[hh][hrows, vc] = (o * _silu_of_twice(g_ref[rows, vc].astype(F32))).astype(BF16)

    def ssd_step(c, g):
        rows = chunk_rows(c)
        hh, hrows = c // _PART_CHUNKS, part_rows(c)
        vc = sval_cols(g)
        s_state[g] = s_state[g] * dec[c:c + 1, vc] + s_upd.pop((c, g))
        y = _dot(bc_ref[rows, out_cols(g)], s_state[g].astype(BF16))
        y = y + d_ref[:, vc] * xs_ref[rows, vc]
        y = y * _silu_of_twice(z_ref[rows, vc].astype(F32))
        y = y * lax.rsqrt(jnp.mean(y * y, axis=-1, keepdims=True) + EPS)
        yb_refs[hh][hrows, vc] = y.astype(BF16)

    def merge_pieces(hh):
        rows = slice(hh * _PART, (hh + 1) * _PART)
        pieces = []
        for j in range(0, D_MODEL, _MERGE_COL_STEP):
            def gate_block(j=j):
                cs = slice(j, j + _MERGE_COL_STEP)
                cs_b = slice(D_MODEL + j, D_MODEL + j + _MERGE_COL_STEP)
                u_a = _dot(oa_refs[hh][...], wa_ref[:, cs])
                u_b = _dot(yb_refs[hh][...], wb_ref[:, cs])
                merged = ((u_a + jnp.tanh(gt_ref[rows, cs].astype(F32)) * u_a)
                          + (u_b + jnp.tanh(gt_ref[rows, cs_b].astype(F32)) * u_b))
                mg_refs[hh][:, cs] = merged.astype(BF16)
            pieces.append(gate_block)
        for j in range(0, D_MODEL, _MERGE_COL_STEP):
            def out_block(j=j):
                cs = slice(j, j + _MERGE_COL_STEP)
                o_ref[rows, cs] = x_ref[rows, cs] + _dot(mg_refs[hh][...], wo_ref[:, cs])
            pieces.append(out_block)
        return pieces

    pending = []
    for part in range(_MERGE_PARTS):
        chunks = range(part * _PART_CHUNKS, (part + 1) * _PART_CHUNKS)
        for unit in units[:_LOOKAHEAD]:
            issue_updates(unit, chunks)
        for ui, unit in enumerate(units):
            if ui + _LOOKAHEAD < len(units):
                issue_updates(units[ui + _LOOKAHEAD], chunks)
            for c in chunks:
                (gla_step if unit[0] == "gla" else ssd_step)(c, unit[1])
            for piece in pending[:1]:
                piece()
            pending = pending[1:]
        pending = pending + merge_pieces(part)
    for piece in pending:
        piece()
    for h in range(GLA_HEADS):
        gs_ref[h] = g_state[h]
    for g in range(SSD_GROUPS):
        ss_ref[g] = s_state[g]


def _mix(x2d, p, small, xs, bc, w_f_up, b_f, gla_norm, dt_bias, a_log, d_skip, ssd_norm,
         gla_w_o, ssd_w_o, w_out, bsz, seq):
    t = SEQ_TILE
    nt = seq // t
    tri = (jnp.arange(CHUNK)[:, None] >= jnp.arange(CHUNK)[None, :]).astype(BF16)
    tri = jnp.concatenate([tri, tri], axis=1)
    wf = jnp.concatenate([w_f_up, jnp.zeros((SMALL_W - GLA_RANK, GLA_KEY), F32)], axis=0).astype(BF16)
    pad_l = jnp.zeros((SM_DT,), F32)
    pad_r = jnp.zeros((SMALL_W - SM_DT - SSD_HEADS,), F32)
    dtb = jnp.concatenate([pad_l, dt_bias, pad_r]).reshape(1, SMALL_W)
    alog = jnp.concatenate([pad_l, a_log, pad_r]).reshape(1, SMALL_W)
    head_of_col = jnp.arange(SSD_INNER, dtype=jnp.int32) // SSD_HEADDIM
    expand = (jnp.arange(SMALL_W, dtype=jnp.int32)[:, None] == head_of_col[None, :] + SM_DT).astype(BF16)
    expand = jnp.concatenate([expand, expand], axis=0)
    d_cols = jnp.repeat(d_skip, SSD_HEADDIM).reshape(1, SSD_INNER)
    wa = (jnp.tile(gla_norm, GLA_HEADS)[:, None] * gla_w_o).astype(BF16)
    wb = (ssd_norm[:, None] * ssd_w_o).astype(BF16)
    wo = (0.5 * w_out).astype(BF16)

    def col(width, off):
        blk = off // width
        return pl.BlockSpec((t, width), lambda b, j: (b * nt + j, blk))

    row = lambda b, j: (b * nt + j, 0)
    return pl.pallas_call(
        _mix_body,
        grid=(bsz, nt),
        in_specs=[col(GLA_KEY, P_Q), col(GLA_KEY, P_K), col(GLA_VAL, P_V), col(GLA_VAL, P_G),
                  col(SSD_INNER, P_Z), col(2 * D_MODEL, P_GATES),
                  pl.BlockSpec((t, SMALL_W), row),
                  pl.BlockSpec((t, SSD_INNER), row),
                  pl.BlockSpec((t, 2 * SSD_BC), row),
                  pl.BlockSpec((t, D_MODEL), row),
                  _resident((SMALL_W, GLA_KEY)), _resident((1, GLA_KEY)),
                  _resident((1, SMALL_W)), _resident((1, SMALL_W)),
                  _resident((2 * SMALL_W, SSD_INNER)), _resident((1, SSD_INNER)),
                  _resident((CHUNK, 2 * CHUNK)),
                  _resident((GLA_VAL, D_MODEL)), _resident((SSD_INNER, D_MODEL)),
                  _resident((D_MODEL, D_MODEL))],
        out_specs=pl.BlockSpec((t, D_MODEL), row),
        out_shape=jax.ShapeDtypeStruct((bsz * seq, D_MODEL), F32),
        scratch_shapes=[pltpu.VMEM((GLA_HEADS, GLA_DK, GLA_DV), F32),
                        pltpu.VMEM((SSD_GROUPS, SSD_STATE, SSD_GROUP_W), F32),
                        pltpu.VMEM((t, SSD_INNER), BF16),
                        *([pltpu.VMEM((_PART, GLA_VAL), BF16)] * _MERGE_PARTS),
                        *([pltpu.VMEM((_PART, SSD_INNER), BF16)] * _MERGE_PARTS),
                        *([pltpu.VMEM((_PART, D_MODEL), BF16)] * _MERGE_PARTS)],
        compiler_params=pltpu.CompilerParams(dimension_semantics=("parallel", "arbitrary"),
                                             vmem_limit_bytes=VMEM_LIMIT),
        name="mix",
    )(p, p, p, p, p, p, small, xs, bc, x2d, wf, b_f.reshape(1, GLA_KEY),
      dtb, alog, expand, d_cols, tri, wa, wb, wo)


def kernel(x, ffn1_norm, ffn1_w_gate, ffn1_w_up, ffn1_w_down, mix_norm, w_in, gla_w_f_up, gla_b_f, gla_norm, gla_w_o, ssd_conv_w, ssd_conv_b, ssd_dt_bias, ssd_a_log, ssd_d, ssd_norm, ssd_w_o, w_out, ffn2_norm, ffn2_w_gate, ffn2_w_up, ffn2_w_down, final_norm):
    bsz, seq, d = x.shape
    depth = ffn1_norm.shape[0]
    assert d == D_MODEL and seq % SEQ_TILE == 0 and depth >= 1
    h = x.reshape(bsz * seq, d)
    for l in range(depth):
        last = l == depth - 1
        h = _ffn(h, ffn1_norm[l], ffn1_w_gate[l], ffn1_w_up[l], ffn1_w_down[l])
        p, small, xs, bc = _in_proj(h, mix_norm[l], w_in[l], ssd_conv_w[l], ssd_conv_b[l], seq)
        h = _mix(h, p, small, xs, bc, gla_w_f_up[l], gla_b_f[l], gla_norm[l], ssd_dt_bias[l],
                 ssd_a_log[l], ssd_d[l], ssd_norm[l], gla_w_o[l], ssd_w_o[l], w_out[l], bsz, seq)
        h = _ffn(h, ffn2_norm[l], ffn2_w_gate[l], ffn2_w_up[l], ffn2_w_down[l],
                 final_w=final_norm if last else None)
    return h.reshape(bsz, seq, d)
```
